```python
import math
import jax, jax.numpy as jnp
from jax import lax
import numpy as np

D_MODEL = 2048
BATCH = 2
SEQ = 8192
DEPTH = 1

D_MIX = D_MODEL
ATTN_WIDTH = D_MIX // 2
SSM_WIDTH = D_MIX - ATTN_WIDTH
DIFF_HEAD_DIM = 64
N_DIFF_HEADS = ATTN_WIDTH // (2 * DIFF_HEAD_DIM)
SSM_GROUP = 16
N_SSM_GROUPS = SSM_WIDTH // SSM_GROUP
SSM_STATE = 64
D_FF = ((8 * D_MODEL) // 3 + 255) // 256 * 256
Q_BLOCK = 128
DT_MIN = 1e-3
DT_MAX = 1e-1
LN_EPS = 1e-5
RMS_EPS = 1e-5
DEEPNORM_ALPHA = (2 * DEPTH) ** 0.25
DEEPNORM_BETA = (8 * DEPTH) ** -0.25
FFN_RES_WEIGHT = 0.5
N_SUBLAYERS = 3
IN_COLS = 3 * ATTN_WIDTH + SSM_WIDTH

kernel_name = 'hymba_style_diffattn_s5_macaron_deepnorm_adaln'


def _layer_norm(h, g, b):
    h32 = h.astype(jnp.float32)
    mu = jnp.mean(h32, -1, keepdims=True)
    var = jnp.mean(jnp.square(h32 - mu), -1, keepdims=True)
    out = (h32 - mu) * lax.rsqrt(var + LN_EPS) * g.astype(jnp.float32) + b.astype(jnp.float32)
    return out.astype(h.dtype)


def _modulate(h, shift, scale):
    return h * (1 + scale[:, None, :]) + shift[:, None, :]


def _swiglu(u, w1, w3, w2):
    return (jax.nn.silu(u @ w1) * (u @ w3)) @ w2


def _diff_attention(q, k, v, lq1, lk1, lq2, lk2, subln_g, lambda_init):
    B, L, _ = q.shape
    H, d = N_DIFF_HEADS, DIFF_HEAD_DIM
    q = q.reshape(B, L, H, 2, d)
    k = k.reshape(B, L, H, 2, d)
    v = v.reshape(B, L, H, 2 * d)
    f32 = jnp.float32
    lam = (jnp.exp(jnp.sum(lq1.astype(f32) * lk1.astype(f32)))
           - jnp.exp(jnp.sum(lq2.astype(f32) * lk2.astype(f32))) + lambda_init)
    scale = 1.0 / math.sqrt(d)
    nb = L // Q_BLOCK
    qb = q.reshape(B, nb, Q_BLOCK, H, 2, d).transpose(1, 0, 2, 3, 4, 5)
    k_pos = jnp.arange(L)

    def block(args):
        q_blk, blk = args
        s = jnp.einsum('bqhcd,bkhcd->bhcqk', q_blk, k).astype(f32) * scale
        q_pos = blk * Q_BLOCK + jnp.arange(Q_BLOCK)
        mask = k_pos[None, :] <= q_pos[:, None]
        s = jnp.where(mask, s, -jnp.inf)
        p = jax.nn.softmax(s, axis=-1)
        w = p[:, :, 0] - lam * p[:, :, 1]
        return jnp.einsum('bhqk,bkhe->bqhe', w.astype(v.dtype), v)

    o = lax.map(block, (qb, jnp.arange(nb)))
    o = o.transpose(1, 0, 2, 3, 4).reshape(B, L, H, 2 * d).astype(f32)
    o = o * lax.rsqrt(jnp.mean(jnp.square(o), -1, keepdims=True) + RMS_EPS)
    o = o * subln_g.astype(f32) * (1.0 - lambda_init)
    return o.reshape(B, L, ATTN_WIDTH).astype(q.dtype)


def _s5_branch(s, a_re, a_im, log_dt, b_re, b_im, c_re, c_im, d_skip, glu_w, glu_b):
    B, L, _ = s.shape
    G, P, Hc = N_SSM_GROUPS, SSM_STATE, SSM_GROUP
    f32 = jnp.float32
    a_re = a_re.astype(f32); a_im = a_im.astype(f32)
    b_re = b_re.astype(f32); b_im = b_im.astype(f32)
    dt = jnp.exp(log_dt.astype(f32))[:, None]
    mag = jnp.exp(a_re * dt)
    lbar_re = mag * jnp.cos(a_im * dt)
    lbar_im = mag * jnp.sin(a_im * dt)
    nr = lbar_re - 1.0
    ni = lbar_im
    den = jnp.square(a_re) + jnp.square(a_im)
    coef_re = ((nr * a_re + ni * a_im) / den)[..., None]
    coef_im = ((ni * a_re - nr * a_im) / den)[..., None]
    bb_re = coef_re * b_re - coef_im * b_im
    bb_im = coef_re * b_im + coef_im * b_re
    u = s.astype(f32).reshape(B, L, G, Hc)
    x_re = jnp.einsum('blgh,gph->lbgp', u, bb_re)
    x_im = jnp.einsum('blgh,gph->lbgp', u, bb_im)
    al_re = jnp.broadcast_to(lbar_re[None, None], (L, 1, G, P))
    al_im = jnp.broadcast_to(lbar_im[None, None], (L, 1, G, P))

    def combine(e1, e2):
        a1r, a1i, b1r, b1i = e1
        a2r, a2i, b2r, b2i = e2
        return (a2r * a1r - a2i * a1i,
                a2r * a1i + a2i * a1r,
                a2r * b1r - a2i * b1i + b2r,
                a2r * b1i + a2i * b1r + b2i)

    _, _, h_re, h_im = lax.associative_scan(combine, (al_re, al_im, x_re, x_im), axis=0)
    y = (jnp.einsum('lbgp,ghp->blgh', h_re, c_re.astype(f32))
         - jnp.einsum('lbgp,ghp->blgh', h_im, c_im.astype(f32)))
    y = y.reshape(B, L, SSM_WIDTH) + d_skip.astype(f32) * u.reshape(B, L, SSM_WIDTH)
    g = jax.nn.gelu(y)
    out = g * jax.nn.sigmoid(g @ glu_w.astype(f32) + glu_b.astype(f32))
    return out.astype(s.dtype)


def _hybrid_mixer(u, w_in, lq1, lk1, lq2, lk2, subln_g, a_re, a_im, log_dt, b_re, b_im,
                  c_re, c_im, d_skip, glu_w, glu_b, w_out, lambda_init):
    proj = u @ w_in
    q, k, v, s = jnp.split(proj, [ATTN_WIDTH, 2 * ATTN_WIDTH, 3 * ATTN_WIDTH], axis=-1)
    attn = _diff_attention(q, k, v, lq1, lk1, lq2, lk2, subln_g, lambda_init)
    ssm = _s5_branch(s, a_re, a_im, log_dt, b_re, b_im, c_re, c_im, d_skip, glu_w, glu_b)
    return jnp.concatenate([attn, ssm], axis=-1) @ w_out


def setup_inputs(seed: int = 0) -> dict:
    key = jax.random.key(seed)
    ks = jax.random.split(key, 40)
    f32 = jnp.float32

    def nrm(k, shape, s):
        return jax.random.normal(k, shape, f32) * s

    D, F, G, P, Hc = D_MODEL, D_FF, N_SSM_GROUPS, SSM_STATE, SSM_GROUP
    n_idx = jnp.arange(P, dtype=f32)
    w_in = jnp.concatenate([
        nrm(ks[8], (DEPTH, D, 2 * ATTN_WIDTH), D ** -0.5),
        nrm(ks[9], (DEPTH, D, ATTN_WIDTH), D ** -0.5) * DEEPNORM_BETA,
        nrm(ks[10], (DEPTH, D, SSM_WIDTH), D ** -0.5)], axis=-1)
    return {
        'x': nrm(ks[0], (BATCH, SEQ, D), 1.0),
        'c': nrm(ks[1], (BATCH, D), 1.0),
        'w_cond': nrm(ks[2], (DEPTH, D, N_SUBLAYERS * 3 * D), 0.5 * D ** -0.5),
        'b_cond': nrm(ks[3], (DEPTH, N_SUBLAYERS * 3 * D), 0.01),
        'ffn1_w1': nrm(ks[4], (DEPTH, D, F), D ** -0.5),
        'ffn1_w3': nrm(ks[5], (DEPTH, D, F), D ** -0.5),
        'ffn1_w2': nrm(ks[6], (DEPTH, F, D), F ** -0.5) * DEEPNORM_BETA,
        'w_in': w_in,
        'lambda_q1': nrm(ks[11], (DEPTH, DIFF_HEAD_DIM), 0.1),
        'lambda_k1': nrm(ks[12], (DEPTH, DIFF_HEAD_DIM), 0.1),
        'lambda_q2': nrm(ks[13], (DEPTH, DIFF_HEAD_DIM), 0.1),
        'lambda_k2': nrm(ks[14], (DEPTH, DIFF_HEAD_DIM), 0.1),
        'subln_g': 1.0 + nrm(ks[15], (DEPTH, 2 * DIFF_HEAD_DIM), 0.01),
        'ssm_a_re': -0.5 + nrm(ks[16], (DEPTH, G, P), 0.01),
        'ssm_a_im': math.pi * n_idx + nrm(ks[17], (DEPTH, G, P), 0.01),
        'ssm_log_dt': jax.random.uniform(ks[18], (DEPTH, G), f32, math.log(DT_MIN), math.log(DT_MAX)),
        'ssm_b_re': nrm(ks[19], (DEPTH, G, P, Hc), (2 * Hc) ** -0.5),
        'ssm_b_im': nrm(ks[20], (DEPTH, G, P, Hc), (2 * Hc) ** -0.5),
        'ssm_c_re': nrm(ks[21], (DEPTH, G, Hc, P), (2 * P) ** -0.5),
        'ssm_c_im': nrm(ks[22], (DEPTH, G, Hc, P), (2 * P) ** -0.5),
        'ssm_d': nrm(ks[23], (DEPTH, SSM_WIDTH), 1.0),
        'glu_w': nrm(ks[24], (DEPTH, SSM_WIDTH, SSM_WIDTH), SSM_WIDTH ** -0.5),
        'glu_b': nrm(ks[25], (DEPTH, SSM_WIDTH), 0.01),
        'w_out': nrm(ks[26], (DEPTH, D_MIX, D), D_MIX ** -0.5) * DEEPNORM_BETA,
        'ffn2_w1': nrm(ks[27], (DEPTH, D, F), D ** -0.5),
        'ffn2_w3': nrm(ks[28], (DEPTH, D, F), D ** -0.5),
        'ffn2_w2': nrm(ks[29], (DEPTH, F, D), F ** -0.5) * DEEPNORM_BETA,
        'ln_g': 1.0 + nrm(ks[30], (DEPTH, N_SUBLAYERS, D), 0.01),
        'ln_b': nrm(ks[31], (DEPTH, N_SUBLAYERS, D), 0.01),
    }


def reference(x, c, w_cond, b_cond, ffn1_w1, ffn1_w3, ffn1_w2, w_in, lambda_q1, lambda_k1,
              lambda_q2, lambda_k2, subln_g, ssm_a_re, ssm_a_im, ssm_log_dt, ssm_b_re, ssm_b_im,
              ssm_c_re, ssm_c_im, ssm_d, glu_w, glu_b, w_out, ffn2_w1, ffn2_w3, ffn2_w2, ln_g, ln_b):
    B = x.shape[0]
    for i in range(DEPTH):
        lambda_init = 0.8 - 0.6 * math.exp(-0.3 * i)
        mod = (jax.nn.silu(c) @ w_cond[i] + b_cond[i]).reshape(B, N_SUBLAYERS, 3, D_MODEL).astype(x.dtype)

        u = _modulate(x, mod[:, 0, 0], mod[:, 0, 1])
        h = _swiglu(u, ffn1_w1[i], ffn1_w3[i], ffn1_w2[i])
        x = _layer_norm(DEEPNORM_ALPHA * x + FFN_RES_WEIGHT * (1 + mod[:, 0, 2])[:, None, :] * h,
                        ln_g[i, 0], ln_b[i, 0])

        u = _modulate(x, mod[:, 1, 0], mod[:, 1, 1])
        h = _hybrid_mixer(u, w_in[i], lambda_q1[i], lambda_k1[i], lambda_q2[i], lambda_k2[i], subln_g[i],
                          ssm_a_re[i], ssm_a_im[i], ssm_log_dt[i], ssm_b_re[i], ssm_b_im[i],
                          ssm_c_re[i], ssm_c_im[i], ssm_d[i], glu_w[i], glu_b[i], w_out[i], lambda_init)
        x = _layer_norm(DEEPNORM_ALPHA * x + (1 + mod[:, 1, 2])[:, None, :] * h, ln_g[i, 1], ln_b[i, 1])

        u = _modulate(x, mod[:, 2, 0], mod[:, 2, 1])
        h = _swiglu(u, ffn2_w1[i], ffn2_w3[i], ffn2_w2[i])
        x = _layer_norm(DEEPNORM_ALPHA * x + FFN_RES_WEIGHT * (1 + mod[:, 2, 2])[:, None, :] * h,
                        ln_g[i, 2], ln_b[i, 2])
    return x
```

```python
import functools
import math

import jax
import jax.numpy as jnp
from jax import lax
from jax.experimental import pallas as pl
from jax.experimental.pallas import tpu as pltpu

F32 = jnp.float32
BF16 = jnp.bfloat16

N_SUBLAYERS = 3
DIFF_HEAD_DIM = 64
SSM_GROUP = 16
SSM_STATE = 64
LN_EPS = 1e-5
RMS_EPS = 1e-5
FFN_RES_WEIGHT = 0.5
SSM_CHUNK = 16
GROUPS_PER_STEP = 2

V7X_VMEM_BYTES = 64 * 1024 * 1024
VMEM_LIMIT = V7X_VMEM_BYTES - 8 * 1024 * 1024
LANES = 128

FFN_TOKEN_TILE = 512
FFN_HIDDEN_TILE = 512
FFN_OUT_CHUNK = 512
ROW_CHUNK = 256
PROJ_TOKEN_TILE = 512
ATTN_Q_TILE = 512
ATTN_KV_TILE = 512
COND_COL_TILE = 1024
NEG_BIG = -1e30


def _params(*sem):
    return pltpu.CompilerParams(dimension_semantics=sem, vmem_limit_bytes=VMEM_LIMIT)


def _const_spec(shape):
    nd = len(shape)
    return pl.BlockSpec(shape, lambda *_: (0,) * nd, pipeline_mode=pl.Buffered(1))


def _layer_norm_rows(y, g, b):
    mu = jnp.mean(y, axis=-1, keepdims=True)
    yc = y - mu
    var = jnp.mean(yc * yc, axis=-1, keepdims=True)
    return yc * lax.rsqrt(var + LN_EPS) * g + b


def _cond_kernel(c_ref, w_ref, b_ref, o_ref):
    c = c_ref[...]
    a = c * jax.nn.sigmoid(c)
    o_ref[...] = jnp.dot(a, w_ref[...], preferred_element_type=F32,
                         precision=lax.Precision.HIGHEST) + b_ref[...]


def _cond(c, w_cond, b_cond):
    bsz, d = c.shape
    ncol = w_cond.shape[1]
    rows = 8
    c_pad = jnp.zeros((rows, d), F32).at[:bsz].set(c)
    tn = COND_COL_TILE
    out = pl.pallas_call(
        _cond_kernel,
        grid=(ncol // tn,),
        in_specs=[pl.BlockSpec((rows, d), lambda j: (0, 0)),
                  pl.BlockSpec((d, tn), lambda j: (0, j)),
                  pl.BlockSpec((1, tn), lambda j: (0, j))],
        out_specs=pl.BlockSpec((rows, tn), lambda j: (0, j)),
        out_shape=jax.ShapeDtypeStruct((rows, ncol), F32),
        compiler_params=_params("arbitrary"),
        name="cond",
    )(c_pad, w_cond, b_cond.reshape(1, ncol))
    return out[:bsz]


def _ffn_kernel(x_ref, sh_ref, sc_ref, gt_ref, w1_ref, w3_ref, w2_ref, g_ref, b_ref,
                o_ref, u_ref, *, alpha, n_hidden_tiles, tm, d):
    j = pl.program_id(1)

    @pl.when(j == 0)
    def _():
        for r in range(0, tm, ROW_CHUNK):
            rows = pl.ds(r, ROW_CHUNK)
            u_ref[rows, :] = (x_ref[rows, :] * (1.0 + sc_ref[0]) + sh_ref[0]).astype(BF16)

    u = u_ref[...]
    h1 = jnp.dot(u, w1_ref[...], preferred_element_type=F32)
    h3 = jnp.dot(u, w3_ref[...], preferred_element_type=F32)
    a = (h1 * jax.nn.sigmoid(h1) * h3).astype(BF16)

    @pl.when(j == 0)
    def _():
        for n in range(0, d, FFN_OUT_CHUNK):
            cols = pl.ds(n, FFN_OUT_CHUNK)
            o_ref[:, cols] = jnp.dot(a, w2_ref[:, cols], preferred_element_type=F32)

    @pl.when(j > 0)
    def _():
        for n in range(0, d, FFN_OUT_CHUNK):
            cols = pl.ds(n, FFN_OUT_CHUNK)
            o_ref[:, cols] += jnp.dot(a, w2_ref[:, cols], preferred_element_type=F32)

    @pl.when(j == n_hidden_tiles - 1)
    def _():
        gate = FFN_RES_WEIGHT * (1.0 + gt_ref[0])
        for r in range(0, tm, ROW_CHUNK):
            rows = pl.ds(r, ROW_CHUNK)
            y = alpha * x_ref[rows, :] + gate * o_ref[rows, :]
            o_ref[rows, :] = _layer_norm_rows(y, g_ref[...], b_ref[...])


def _ffn(x2d, shift, scale, gate, w1, w3, w2, ln_g, ln_b, *, seq, alpha):
    n, d = x2d.shape
    f = w1.shape[1]
    tm = min(FFN_TOKEN_TILE, seq)
    tf = FFN_HIDDEN_TILE
    tiles_per_seq = seq // tm
    nj = f // tf
    mod_spec = pl.BlockSpec((1, 1, d), lambda i, j: (i // tiles_per_seq, 0, 0))
    vec_spec = pl.BlockSpec((1, d), lambda i, j: (0, 0))
    kern = functools.partial(_ffn_kernel, alpha=alpha, n_hidden_tiles=nj, tm=tm, d=d)
    return pl.pallas_call(
        kern,
        grid=(n // tm, nj),
        in_specs=[pl.BlockSpec((tm, d), lambda i, j: (i, 0)),
                  mod_spec, mod_spec, mod_spec,
                  pl.BlockSpec((d, tf), lambda i, j: (0, j)),
                  pl.BlockSpec((d, tf), lambda i, j: (0, j)),
                  pl.BlockSpec((tf, d), lambda i, j: (j, 0)),
                  vec_spec, vec_spec],
        out_specs=pl.BlockSpec((tm, d), lambda i, j: (i, 0)),
        out_shape=jax.ShapeDtypeStruct((n, d), F32),
        scratch_shapes=[pltpu.VMEM((tm, d), BF16)],
        compiler_params=_params("parallel", "arbitrary"),
        name="ffn",
    )(x2d, shift, scale, gate, w1, w3, w2, ln_g.reshape(1, d), ln_b.reshape(1, d))


def _inproj_kernel(x_ref, sh_ref, sc_ref, w_ref, q_ref, k1_ref, k2_ref, v_ref, s_ref, *, width):
    u = (x_ref[...] * (1.0 + sc_ref[0]) + sh_ref[0]).astype(BF16)
    qk_scale = 1.0 / math.sqrt(DIFF_HEAD_DIM)
    q = jnp.dot(u, w_ref[:, 0:width], preferred_element_type=F32)
    q_ref[...] = (q * qk_scale).astype(BF16)
    k = jnp.dot(u, w_ref[:, width:2 * width], preferred_element_type=F32)
    lane = lax.broadcasted_iota(jnp.int32, k.shape, 1)
    first = (lane & DIFF_HEAD_DIM) == 0
    k1_ref[...] = jnp.where(first, k, 0.0).astype(BF16)
    k2_ref[...] = jnp.where(first, 0.0, k).astype(BF16)
    v_ref[...] = jnp.dot(u, w_ref[:, 2 * width:3 * width], preferred_element_type=F32).astype(BF16)
    s_ref[...] = jnp.dot(u, w_ref[:, 3 * width:4 * width], preferred_element_type=F32)


def _inproj(x2d, shift, scale, w_in, *, seq):
    n, d = x2d.shape
    width = w_in.shape[1] // 4
    tm = min(PROJ_TOKEN_TILE, seq)
    tiles_per_seq = seq // tm
    mod_spec = pl.BlockSpec((1, 1, d), lambda i: (i // tiles_per_seq, 0, 0))
    out_spec = pl.BlockSpec((tm, width), lambda i: (i, 0))
    bf = jax.ShapeDtypeStruct((n, width), BF16)
    return pl.pallas_call(
        functools.partial(_inproj_kernel, width=width),
        grid=(n // tm,),
        in_specs=[pl.BlockSpec((tm, d), lambda i: (i, 0)), mod_spec, mod_spec,
                  _const_spec(w_in.shape)],
        out_specs=[out_spec] * 5,
        out_shape=[bf, bf, bf, bf, jax.ShapeDtypeStruct((n, width), F32)],
        compiler_params=_params("parallel"),
        name="inproj",
    )(x2d, shift, scale, w_in)


def _attn_kernel(q_ref, k1_ref, k2_ref, v_ref, lq1_ref, lk1_ref, lq2_ref, lk2_ref, g_ref,
                 o_ref, *, tq, tk, lambda_init):
    qi = pl.program_id(2)
    q = q_ref[...]
    nt = (((1,), (1,)), ((), ()))

    def softmax_update(s, m, l, acc, v):
        m_new = jnp.maximum(m, jnp.max(s, axis=-1, keepdims=True))
        corr = jnp.exp(m - m_new)
        p = jnp.exp(s - m_new)
        l_new = corr * l + jnp.sum(p, axis=-1, keepdims=True)
        acc_new = corr * acc + jnp.dot(p.astype(BF16), v, preferred_element_type=F32)
        return m_new, l_new, acc_new

    def block(kb, carry, masked):
        m1, l1, a1, m2, l2, a2 = carry
        rows = pl.ds(pl.multiple_of(kb * tk, tk), tk)
        v = v_ref[rows, :]
        s1 = lax.dot_general(q, k1_ref[rows, :], nt, preferred_element_type=F32)
        s2 = lax.dot_general(q, k2_ref[rows, :], nt, preferred_element_type=F32)
        if masked:
            r = lax.broadcasted_iota(jnp.int32, (tq, tk), 0)
            c = lax.broadcasted_iota(jnp.int32, (tq, tk), 1)
            keep = c <= r
            s1 = jnp.where(keep, s1, -jnp.inf)
            s2 = jnp.where(keep, s2, -jnp.inf)
        m1, l1, a1 = softmax_update(s1, m1, l1, a1, v)
        m2, l2, a2 = softmax_update(s2, m2, l2, a2, v)
        return m1, l1, a1, m2, l2, a2

    stat = jnp.full((tq, 1), NEG_BIG, F32)
    zero = jnp.zeros((tq, 1), F32)
    acc0 = jnp.zeros((tq, LANES), F32)
    init = (stat, zero, acc0, stat, zero, acc0)
    carry = lax.fori_loop(0, qi, lambda kb, cr: block(kb, cr, False), init)
    m1, l1, a1, m2, l2, a2 = block(qi, carry, True)

    lam = (jnp.exp(jnp.sum(lq1_ref[...] * lk1_ref[...], axis=-1, keepdims=True))
           - jnp.exp(jnp.sum(lq2_ref[...] * lk2_ref[...], axis=-1, keepdims=True)) + lambda_init)
    o = a1 / l1 - lam * (a2 / l2)
    o = o * lax.rsqrt(jnp.mean(o * o, axis=-1, keepdims=True) + RMS_EPS)
    o_ref[...] = (o * g_ref[...] * (1.0 - lambda_init)).astype(BF16)


def _attention(q, k1, k2, v, lq1, lk1, lq2, lk2, subln_g, *, batch, seq, lambda_init):
    n, width = q.shape
    heads = width // LANES
    tq = min(ATTN_Q_TILE, seq)
    tk = tq
    nq = seq // tq
    kv_spec = pl.BlockSpec((seq, LANES), lambda b, h, i: (b, h))
    lam_spec = pl.BlockSpec((1, DIFF_HEAD_DIM), lambda b, h, i: (0, 0))
    q_spec = pl.BlockSpec((tq, LANES), lambda b, h, i: (b * nq + i, h))
    kern = functools.partial(_attn_kernel, tq=tq, tk=tk, lambda_init=lambda_init)
    row = lambda a: a.reshape(1, -1).astype(F32)
    return pl.pallas_call(
        kern,
        grid=(batch, heads, nq),
        in_specs=[q_spec, kv_spec, kv_spec, kv_spec, lam_spec, lam_spec, lam_spec, lam_spec,
                  pl.BlockSpec((1, LANES), lambda b, h, i: (0, 0))],
        out_specs=q_spec,
        out_shape=jax.ShapeDtypeStruct((n, width), BF16),
        compiler_params=_params("parallel", "parallel", "arbitrary"),
        name="attn",
    )(q, k1, k2, v, row(lq1), row(lk1), row(lq2), row(lk2), row(subln_g))


def _ssm_operators(a_re, a_im, log_dt, b_re, b_im, c_re, c_im):
    hp = lax.Precision.HIGHEST
    t_len, g_n, p_n, h_n = SSM_CHUNK, a_re.shape[0], a_re.shape[1], b_re.shape[2]
    dt = jnp.exp(log_dt)[:, None]
    mag = jnp.exp(a_re * dt)
    lbar_re = mag * jnp.cos(a_im * dt)
    lbar_im = mag * jnp.sin(a_im * dt)
    nr, ni = lbar_re - 1.0, lbar_im
    den = jnp.square(a_re) + jnp.square(a_im)
    coef_re = ((nr * a_re + ni * a_im) / den)[..., None]
    coef_im = ((ni * a_re - nr * a_im) / den)[..., None]
    bb_re = coef_re * b_re - coef_im * b_im
    bb_im = coef_re * b_im + coef_im * b_re
    ks = jnp.arange(t_len + 1, dtype=F32)[:, None, None]
    mag_k = jnp.exp(a_re * dt * ks)
    pw_re = mag_k * jnp.cos(a_im * dt * ks)
    pw_im = mag_k * jnp.sin(a_im * dt * ks)
    cp_re = c_re[None] * pw_re[:, :, None, :] - c_im[None] * pw_im[:, :, None, :]
    cp_im = c_re[None] * pw_im[:, :, None, :] + c_im[None] * pw_re[:, :, None, :]
    kern = (jnp.einsum('tgop,gph->gtoh', cp_re, bb_re, precision=hp)
            - jnp.einsum('tgop,gph->gtoh', cp_im, bb_im, precision=hp))
    s_idx = jnp.arange(t_len)[:, None]
    t_idx = jnp.arange(t_len)[None, :]
    lag = t_idx - s_idx
    toep = kern[:, jnp.clip(lag, 0, t_len)]
    toep = jnp.where((lag >= 0)[None, :, :, None, None], toep, 0.0)
    intra = toep.transpose(0, 1, 4, 2, 3).reshape(g_n, t_len * h_n, t_len * h_n)
    rev_re = pw_re[t_len - 1::-1][:t_len]
    rev_im = pw_im[t_len - 1::-1][:t_len]
    into_re = (rev_re[:, :, :, None] * bb_re[None] - rev_im[:, :, :, None] * bb_im[None])
    into_im = (rev_re[:, :, :, None] * bb_im[None] + rev_im[:, :, :, None] * bb_re[None])
    into_re = into_re.transpose(1, 0, 3, 2).reshape(g_n, t_len * h_n, p_n)
    into_im = into_im.transpose(1, 0, 3, 2).reshape(g_n, t_len * h_n, p_n)
    outof_re = cp_re[1:].transpose(1, 3, 0, 2).reshape(g_n, p_n, t_len * h_n)
    outof_im = (-cp_im[1:]).transpose(1, 3, 0, 2).reshape(g_n, p_n, t_len * h_n)

    def pair_diag(m):
        gp = m.shape[0] // GROUPS_PER_STEP
        r, c = m.shape[1], m.shape[2]
        m = m.reshape(gp, GROUPS_PER_STEP, r, c)
        eye = jnp.eye(GROUPS_PER_STEP, dtype=m.dtype)
        return jnp.einsum('gark,ab->garbk', m, eye).reshape(gp, GROUPS_PER_STEP * r, GROUPS_PER_STEP * c)

    gp = g_n // GROUPS_PER_STEP
    decay_re = pw_re[t_len].reshape(gp, 1, GROUPS_PER_STEP * p_n)
    decay_im = pw_im[t_len].reshape(gp, 1, GROUPS_PER_STEP * p_n)
    return (intra.astype(BF16), pair_diag(into_re).astype(BF16), pair_diag(into_im).astype(BF16),
            pair_diag(outof_re).astype(BF16), pair_diag(outof_im).astype(BF16), decay_re, decay_im)


def _ssm_kernel(u_ref, intra_ref, into_re_ref, into_im_ref, outof_re_ref, outof_im_ref,
                dec_re_ref, dec_im_ref, y_ref, loc_re, loc_im, ent_re, ent_im,
                *, batch, chunks_per_seq, cols_per_group):
    ub = u_ref[...].astype(BF16)
    loc_re[...] = jnp.dot(ub, into_re_ref[0], preferred_element_type=F32)
    loc_im[...] = jnp.dot(ub, into_im_ref[0], preferred_element_type=F32)
    dre = dec_re_ref[0]
    dim = dec_im_ref[0]

    def step(c, carry):
        new = []
        for b in range(batch):
            sre, sim = carry[2 * b], carry[2 * b + 1]
            row = pl.ds(b * chunks_per_seq + c, 1)
            ent_re[row, :] = sre
            ent_im[row, :] = sim
            new.append(dre * sre - dim * sim + loc_re[row, :])
            new.append(dre * sim + dim * sre + loc_im[row, :])
        return tuple(new)

    zero = jnp.zeros((1, LANES), F32)
    lax.fori_loop(0, chunks_per_seq, step, (zero,) * (2 * batch))

    inter = (jnp.dot(ent_re[...].astype(BF16), outof_re_ref[0], preferred_element_type=F32)
             + jnp.dot(ent_im[...].astype(BF16), outof_im_ref[0], preferred_element_type=F32))
    for gl in range(GROUPS_PER_STEP):
        cols = pl.ds(gl * cols_per_group, cols_per_group)
        y_ref[:, cols] = inter[:, gl * cols_per_group:(gl + 1) * cols_per_group] + jnp.dot(
            ub[:, gl * cols_per_group:(gl + 1) * cols_per_group], intra_ref[gl],
            preferred_element_type=F32)


def _ssm(s2d, ops, *, batch, seq):
    intra, into_re, into_im, outof_re, outof_im, dec_re, dec_im = ops
    n, width = s2d.shape
    t_len, h_n = SSM_CHUNK, SSM_GROUP
    g_n = width // h_n
    chunks_per_seq = seq // t_len
    rows = batch * chunks_per_seq
    cpg = t_len * h_n
    step_cols = GROUPS_PER_STEP * cpg
    u = s2d.reshape(batch, chunks_per_seq, t_len, g_n, h_n).transpose(0, 1, 3, 2, 4).reshape(rows, g_n * cpg)
    state_cols = GROUPS_PER_STEP * SSM_STATE
    blk = lambda shape: pl.BlockSpec((1,) + shape, lambda g: (g, 0, 0))
    kern = functools.partial(_ssm_kernel, batch=batch, chunks_per_seq=chunks_per_seq, cols_per_group=cpg)
    y = pl.pallas_call(
        kern,
        grid=(g_n // GROUPS_PER_STEP,),
        in_specs=[pl.BlockSpec((rows, step_cols), lambda g: (0, g)),
                  pl.BlockSpec((GROUPS_PER_STEP, cpg, cpg), lambda g: (g, 0, 0)),
                  blk((step_cols, state_cols)), blk((step_cols, state_cols)),
                  blk((state_cols, step_cols)), blk((state_cols, step_cols)),
                  blk((1, state_cols)), blk((1, state_cols))],
        out_specs=pl.BlockSpec((rows, step_cols), lambda g: (0, g)),
        out_shape=jax.ShapeDtypeStruct((rows, g_n * cpg), F32),
        scratch_shapes=[pltpu.VMEM((rows, state_cols), F32)] * 4,
        compiler_params=_params("parallel"),
        name="ssm",
    )(u, intra, into_re, into_im, outof_re, outof_im, dec_re, dec_im)
    return y.reshape(batch, chunks_per_seq, g_n, t_len, h_n).transpose(0, 1, 3, 2, 4).reshape(n, width)


def _gelu_tanh(x):
    return 0.5 * x * (1.0 + jnp.tanh(math.sqrt(2.0 / math.pi) * (x + 0.044715 * (x * x * x))))


def _mixout_kernel(attn_ref, y_ref, s_ref, x_ref, gt_ref, dskip_ref, gluw_ref, glub_ref, wout_ref,
                   g_ref, b_ref, o_ref, *, alpha, width):
    y = y_ref[...] + dskip_ref[...] * s_ref[...]
    ge = _gelu_tanh(y)
    z = jnp.dot(ge.astype(BF16), gluw_ref[...], preferred_element_type=F32) + glub_ref[...]
    ssm = (ge * jax.nn.sigmoid(z)).astype(BF16)
    h = (jnp.dot(attn_ref[...], wout_ref[0:width, :], preferred_element_type=F32)
         + jnp.dot(ssm, wout_ref[width:2 * width, :], preferred_element_type=F32))
    out = alpha * x_ref[...] + (1.0 + gt_ref[0]) * h
    o_ref[...] = _layer_norm_rows(out, g_ref[...], b_ref[...])


def _mixout(attn, y, s2d, x2d, gate, d_skip, glu_w, glu_b, w_out, ln_g, ln_b, *, seq, alpha):
    n, d = x2d.shape
    width = attn.shape[1]
    tm = min(ROW_CHUNK, seq)
    tiles_per_seq = seq // tm
    half = pl.BlockSpec((tm, width), lambda i: (i, 0))
    full = pl.BlockSpec((tm, d), lambda i: (i, 0))
    return pl.pallas_call(
        functools.partial(_mixout_kernel, alpha=alpha, width=width),
        grid=(n // tm,),
        in_specs=[half, half, half, full,
                  pl.BlockSpec((1, 1, d), lambda i: (i // tiles_per_seq, 0, 0)),
                  _const_spec((1, width)), _const_spec(glu_w.shape), _const_spec((1, width)),
                  _const_spec(w_out.shape), _const_spec((1, d)), _const_spec((1, d))],
        out_specs=full,
        out_shape=jax.ShapeDtypeStruct((n, d), F32),
        compiler_params=_params("parallel"),
        name="mixout",
    )(attn, y, s2d, x2d, gate, d_skip.reshape(1, width), glu_w, glu_b.reshape(1, width), w_out,
      ln_g.reshape(1, d), ln_b.reshape(1, d))


def kernel(x, c, w_cond, b_cond, ffn1_w1, ffn1_w3, ffn1_w2, w_in, lambda_q1, lambda_k1, lambda_q2, lambda_k2, subln_g, ssm_a_re, ssm_a_im, ssm_log_dt, ssm_b_re, ssm_b_im, ssm_c_re, ssm_c_im, ssm_d, glu_w, glu_b, w_out, ffn2_w1, ffn2_w3, ffn2_w2, ln_g, ln_b):
    batch, seq, d = x.shape
    depth = w_cond.shape[0]
    alpha = (2 * depth) ** 0.25
    h = x.reshape(batch * seq, d)
    for i in range(depth):
        lambda_init = 0.8 - 0.6 * math.exp(-0.3 * i)
        mod = _cond(c, w_cond[i], b_cond[i]).reshape(batch, N_SUBLAYERS, 3, 1, d)
        shift, scale, gate = (lambda k: mod[:, k, 0]), (lambda k: mod[:, k, 1]), (lambda k: mod[:, k, 2])

        h = _ffn(h, shift(0), scale(0), gate(0), ffn1_w1[i].astype(BF16), ffn1_w3[i].astype(BF16),
                 ffn1_w2[i].astype(BF16), ln_g[i, 0], ln_b[i, 0], seq=seq, alpha=alpha)

        q, k1, k2, v, s = _inproj(h, shift(1), scale(1), w_in[i].astype(BF16), seq=seq)
        attn = _attention(q, k1, k2, v, lambda_q1[i], lambda_k1[i], lambda_q2[i], lambda_k2[i],
                          subln_g[i], batch=batch, seq=seq, lambda_init=lambda_init)
        ops = _ssm_operators(ssm_a_re[i], ssm_a_im[i], ssm_log_dt[i], ssm_b_re[i], ssm_b_im[i],
                             ssm_c_re[i], ssm_c_im[i])
        y = _ssm(s, ops, batch=batch, seq=seq)
        h = _mixout(attn, y, s, h, gate(1), ssm_d[i], glu_w[i].astype(BF16), glu_b[i],
                    w_out[i].astype(BF16), ln_g[i, 1], ln_b[i, 1], seq=seq, alpha=alpha)

        h = _ffn(h, shift(2), scale(2), gate(2), ffn2_w1[i].astype(BF16), ffn2_w3[i].astype(BF16),
                 ffn2_w2[i].astype(BF16), ln_g[i, 2], ln_b[i, 2], seq=seq, alpha=alpha)
    return h.reshape(batch, seq, d)
```

```python
import functools
import math

import jax
import jax.numpy as jnp
from jax import lax
from jax.experimental import pallas as pl
from jax.experimental.pallas import tpu as pltpu

F32 = jnp.float32
BF16 = jnp.bfloat16

N_SUBLAYERS = 3
DIFF_HEAD_DIM = 64
SSM_GROUP = 16
SSM_STATE = 64
LN_EPS = 1e-5
RMS_EPS = 1e-5
FFN_RES_WEIGHT = 0.5
SSM_CHUNK = 16
SSM_TOKEN_TILE = 1024
SSM_COL_TILE = 256

V7X_VMEM_BYTES = 64 * 1024 * 1024
VMEM_LIMIT = V7X_VMEM_BYTES - 8 * 1024 * 1024
LANES = 128

FFN_TOKEN_TILE = 512
FFN_HIDDEN_TILE = 512
FFN_OUT_CHUNK = 512
ROW_CHUNK = 256
PROJ_TOKEN_TILE = 512
ATTN_TILE = 512
ATTN_ROW_STRIP = 64
LOG2E = 1.4426950408889634
COND_COL_TILE = 1024
NEG_BIG = -1e30


def _params(*sem):
    return pltpu.CompilerParams(dimension_semantics=sem, vmem_limit_bytes=VMEM_LIMIT)


def _const_spec(shape):
    nd = len(shape)
    return pl.BlockSpec(shape, lambda *_: (0,) * nd, pipeline_mode=pl.Buffered(1))


def _layer_norm_rows(y, g, b):
    mu = jnp.mean(y, axis=-1, keepdims=True)
    yc = y - mu
    var = jnp.mean(yc * yc, axis=-1, keepdims=True)
    return yc * lax.rsqrt(var + LN_EPS) * g + b


def _cond_kernel(c_ref, w_ref, b_ref, o_ref):
    c = c_ref[...]
    a = c * jax.nn.sigmoid(c)
    o_ref[...] = jnp.dot(a, w_ref[...], preferred_element_type=F32,
                         precision=lax.Precision.HIGHEST) + b_ref[...]


def _cond(c, w_cond, b_cond):
    bsz, d = c.shape
    ncol = w_cond.shape[1]
    rows = 8
    c_pad = jnp.zeros((rows, d), F32).at[:bsz].set(c)
    tn = COND_COL_TILE
    out = pl.pallas_call(
        _cond_kernel,
        grid=(ncol // tn,),
        in_specs=[pl.BlockSpec((rows, d), lambda j: (0, 0)),
                  pl.BlockSpec((d, tn), lambda j: (0, j)),
                  pl.BlockSpec((1, tn), lambda j: (0, j))],
        out_specs=pl.BlockSpec((rows, tn), lambda j: (0, j)),
        out_shape=jax.ShapeDtypeStruct((rows, ncol), F32),
        compiler_params=_params("arbitrary"),
        name="cond",
    )(c_pad, w_cond, b_cond.reshape(1, ncol))
    return out[:bsz]


def _ffn_kernel(x_ref, sh_ref, sc_ref, gt_ref, w1_ref, w3_ref, w2_ref, g_ref, b_ref,
                o_ref, u_ref, *, alpha, n_hidden_tiles, tm, d):
    j = pl.program_id(1)

    @pl.when(j == 0)
    def _():
        for r in range(0, tm, ROW_CHUNK):
            rows = pl.ds(r, ROW_CHUNK)
            u_ref[rows, :] = (x_ref[rows, :] * (1.0 + sc_ref[0]) + sh_ref[0]).astype(BF16)

    u = u_ref[...]
    h1 = jnp.dot(u, w1_ref[...], preferred_element_type=F32)
    h3 = jnp.dot(u, w3_ref[...], preferred_element_type=F32)
    a = (h1 * jax.nn.sigmoid(h1) * h3).astype(BF16)

    @pl.when(j == 0)
    def _():
        for n in range(0, d, FFN_OUT_CHUNK):
            cols = pl.ds(n, FFN_OUT_CHUNK)
            o_ref[:, cols] = jnp.dot(a, w2_ref[:, cols], preferred_element_type=F32)

    @pl.when(j > 0)
    def _():
        for n in range(0, d, FFN_OUT_CHUNK):
            cols = pl.ds(n, FFN_OUT_CHUNK)
            o_ref[:, cols] += jnp.dot(a, w2_ref[:, cols], preferred_element_type=F32)

    @pl.when(j == n_hidden_tiles - 1)
    def _():
        gate = FFN_RES_WEIGHT * (1.0 + gt_ref[0])
        for r in range(0, tm, ROW_CHUNK):
            rows = pl.ds(r, ROW_CHUNK)
            y = alpha * x_ref[rows, :] + gate * o_ref[rows, :]
            o_ref[rows, :] = _layer_norm_rows(y, g_ref[...], b_ref[...])


def _ffn(x2d, shift, scale, gate, w1, w3, w2, ln_g, ln_b, *, seq, alpha):
    n, d = x2d.shape
    f = w1.shape[1]
    tm = min(FFN_TOKEN_TILE, seq)
    tf = FFN_HIDDEN_TILE
    tiles_per_seq = seq // tm
    nj = f // tf
    mod_spec = pl.BlockSpec((1, 1, d), lambda i, j: (i // tiles_per_seq, 0, 0))
    vec_spec = pl.BlockSpec((1, d), lambda i, j: (0, 0))
    kern = functools.partial(_ffn_kernel, alpha=alpha, n_hidden_tiles=nj, tm=tm, d=d)
    return pl.pallas_call(
        kern,
        grid=(n // tm, nj),
        in_specs=[pl.BlockSpec((tm, d), lambda i, j: (i, 0)),
                  mod_spec, mod_spec, mod_spec,
                  pl.BlockSpec((d, tf), lambda i, j: (0, j)),
                  pl.BlockSpec((d, tf), lambda i, j: (0, j)),
                  pl.BlockSpec((tf, d), lambda i, j: (j, 0)),
                  vec_spec, vec_spec],
        out_specs=pl.BlockSpec((tm, d), lambda i, j: (i, 0)),
        out_shape=jax.ShapeDtypeStruct((n, d), F32),
        scratch_shapes=[pltpu.VMEM((tm, d), BF16)],
        compiler_params=_params("parallel", "arbitrary"),
        name="ffn",
    )(x2d, shift, scale, gate, w1, w3, w2, ln_g.reshape(1, d), ln_b.reshape(1, d))


def _inproj_kernel(x_ref, sh_ref, sc_ref, w_ref, q_ref, k1_ref, k2_ref, v_ref, s_ref, *, width):
    u = (x_ref[...] * (1.0 + sc_ref[0]) + sh_ref[0]).astype(BF16)
    qk_scale = LOG2E / math.sqrt(DIFF_HEAD_DIM)
    q = jnp.dot(u, w_ref[:, 0:width], preferred_element_type=F32)
    q_ref[...] = (q * qk_scale).astype(BF16)
    k = jnp.dot(u, w_ref[:, width:2 * width], preferred_element_type=F32)
    lane = lax.broadcasted_iota(jnp.int32, k.shape, 1)
    first = (lane & DIFF_HEAD_DIM) == 0
    k1_ref[...] = jnp.where(first, k, 0.0).astype(BF16)
    k2_ref[...] = jnp.where(first, 0.0, k).astype(BF16)
    v_ref[...] = jnp.dot(u, w_ref[:, 2 * width:3 * width], preferred_element_type=F32).astype(BF16)
    s_ref[...] = jnp.dot(u, w_ref[:, 3 * width:4 * width], preferred_element_type=F32)


def _inproj(x2d, shift, scale, w_in, *, seq):
    n, d = x2d.shape
    width = w_in.shape[1] // 4
    tm = min(PROJ_TOKEN_TILE, seq)
    tiles_per_seq = seq // tm
    mod_spec = pl.BlockSpec((1, 1, d), lambda i: (i // tiles_per_seq, 0, 0))
    out_spec = pl.BlockSpec((tm, width), lambda i: (i, 0))
    bf = jax.ShapeDtypeStruct((n, width), BF16)
    return pl.pallas_call(
        functools.partial(_inproj_kernel, width=width),
        grid=(n // tm,),
        in_specs=[pl.BlockSpec((tm, d), lambda i: (i, 0)), mod_spec, mod_spec,
                  _const_spec(w_in.shape)],
        out_specs=[out_spec] * 5,
        out_shape=[bf, bf, bf, bf, jax.ShapeDtypeStruct((n, width), F32)],
        compiler_params=_params("parallel"),
        name="inproj",
    )(x2d, shift, scale, w_in)


def _attn_kernel(q_ref, k1_ref, k2_ref, v_ref, lq1_ref, lk1_ref, lq2_ref, lk2_ref, g_ref,
                 o_ref, sa_scr, sb_scr, p_scr, m_scr, l_scr, acc_scr, *, tq, tk, lambda_init):
    qi = pl.program_id(2)
    nt = (((1,), (1,)), ((), ()))
    k_refs = (k1_ref, k2_ref)
    m_scr[...] = jnp.full(m_scr.shape, NEG_BIG, F32)
    l_scr[...] = jnp.zeros(l_scr.shape, F32)
    acc_scr[...] = jnp.zeros(acc_scr.shape, F32)
    ones = jnp.ones((tk, LANES), BF16)

    def scores(kb, s_scr):
        rows_k = pl.ds(pl.multiple_of(kb * tk, tk), tk)
        q = q_ref[...]
        for mp in range(2):
            s_scr[mp] = lax.dot_general(q, k_refs[mp][rows_k, :], nt, preferred_element_type=F32)

    def consume(kb, s_scr, masked):
        rows_k = pl.ds(pl.multiple_of(kb * tk, tk), tk)
        corrs = []
        for mp in range(2):
            s = s_scr[mp]
            if masked:
                r = lax.broadcasted_iota(jnp.int32, (tq, tk), 0)
                c = lax.broadcasted_iota(jnp.int32, (tq, tk), 1)
                s = jnp.where(c <= r, s, -jnp.inf)
                s_scr[mp] = s
            m_old = m_scr[mp]
            m_new = jnp.maximum(m_old, jnp.broadcast_to(jnp.max(s, axis=-1, keepdims=True), (tq, LANES)))
            m_scr[mp] = m_new
            corrs.append(jnp.exp2(m_old - m_new))
        for mp in range(2):
            for r0 in range(0, tq, ATTN_ROW_STRIP):
                rs = pl.ds(r0, ATTN_ROW_STRIP)
                p = jnp.exp2(s_scr[mp, rs, :] - pltpu.repeat(m_scr[mp, rs, :], tk // LANES, axis=1))
                p_scr[mp, rs, :] = p.astype(BF16)
        v_ext = jnp.concatenate([v_ref[rows_k, :], ones], axis=1)
        for mp in range(2):
            res = jnp.dot(p_scr[mp], v_ext, preferred_element_type=F32)
            acc_scr[mp] = corrs[mp] * acc_scr[mp] + res[:, :LANES]
            l_scr[mp] = corrs[mp] * l_scr[mp] + res[:, LANES:]

    scores(0, sa_scr)

    def body(j, c):
        scores(2 * j + 1, sb_scr)
        consume(2 * j, sa_scr, False)
        scores(2 * j + 2, sa_scr)
        consume(2 * j + 1, sb_scr, False)
        return c
    lax.fori_loop(0, qi // 2, body, 0)

    @pl.when(qi % 2 == 0)
    def _():
        consume(qi, sa_scr, True)

    @pl.when(qi % 2 == 1)
    def _():
        scores(qi, sb_scr)
        consume(qi - 1, sa_scr, False)
        consume(qi, sb_scr, True)

    lam = (jnp.exp(jnp.sum(lq1_ref[...] * lk1_ref[...], axis=-1, keepdims=True))
           - jnp.exp(jnp.sum(lq2_ref[...] * lk2_ref[...], axis=-1, keepdims=True)) + lambda_init)
    o = acc_scr[0] / l_scr[0] - lam * (acc_scr[1] / l_scr[1])
    o = o * lax.rsqrt(jnp.mean(o * o, axis=-1, keepdims=True) + RMS_EPS)
    o_ref[...] = (o * g_ref[...] * (1.0 - lambda_init)).astype(BF16)


def _attention(q, k1, k2, v, lq1, lk1, lq2, lk2, subln_g, *, batch, seq, lambda_init):
    n, width = q.shape
    heads = width // LANES
    tq = min(ATTN_TILE, seq)
    tk = tq
    nq = seq // tq
    kv_spec = pl.BlockSpec((seq, LANES), lambda b, h, i: (b, h))
    lam_spec = pl.BlockSpec((1, DIFF_HEAD_DIM), lambda b, h, i: (0, 0))
    q_spec = pl.BlockSpec((tq, LANES), lambda b, h, i: (b * nq + i, h))
    kern = functools.partial(_attn_kernel, tq=tq, tk=tk, lambda_init=lambda_init)
    row = lambda a: a.reshape(1, -1).astype(F32)
    score_buf = pltpu.VMEM((2, tq, tk), F32)
    stat_buf = pltpu.VMEM((2, tq, LANES), F32)
    return pl.pallas_call(
        kern,
        grid=(batch, heads, nq),
        in_specs=[q_spec, kv_spec, kv_spec, kv_spec, lam_spec, lam_spec, lam_spec, lam_spec,
                  pl.BlockSpec((1, LANES), lambda b, h, i: (0, 0))],
        out_specs=q_spec,
        out_shape=jax.ShapeDtypeStruct((n, width), BF16),
        scratch_shapes=[score_buf, score_buf, pltpu.VMEM((2, tq, tk), BF16),
                        stat_buf, stat_buf, stat_buf],
        compiler_params=_params("parallel", "parallel", "arbitrary"),
        name="attn",
    )(q, k1, k2, v, row(lq1), row(lk1), row(lq2), row(lk2), row(subln_g))


def _ssm_weights(a_re, a_im, log_dt, b_re, b_im, c_re, c_im):
    g_n, p_n, h_n = b_re.shape
    gpb = SSM_COL_TILE // h_n
    nb = g_n // gpb
    dt = jnp.exp(log_dt)[:, None]
    mag = jnp.exp(a_re * dt)
    lbar_re = mag * jnp.cos(a_im * dt)
    lbar_im = mag * jnp.sin(a_im * dt)
    nr, ni = lbar_re - 1.0, lbar_im
    den = jnp.square(a_re) + jnp.square(a_im)
    coef_re = ((nr * a_re + ni * a_im) / den)[..., None]
    coef_im = ((ni * a_re - nr * a_im) / den)[..., None]
    bb_re = coef_re * b_re - coef_im * b_im
    bb_im = coef_re * b_im + coef_im * b_re
    eye = jnp.eye(gpb, dtype=F32)

    def in_diag(bb):
        m = bb.reshape(nb, gpb, p_n, h_n).transpose(0, 1, 3, 2)
        return (m[:, :, :, None, :] * eye[None, :, None, :, None]).reshape(nb, gpb * h_n, gpb * p_n)

    def out_diag(c):
        m = c.reshape(nb, gpb, h_n, p_n).transpose(0, 1, 3, 2)
        return (m[:, :, :, None, :] * eye[None, :, None, :, None]).reshape(nb, gpb * p_n, gpb * h_n)

    lam = lambda v: v.reshape(nb, 1, gpb * p_n)
    return (in_diag(bb_re).astype(BF16), in_diag(bb_im).astype(BF16),
            out_diag(c_re).astype(BF16), out_diag(-c_im).astype(BF16), lam(lbar_re), lam(lbar_im))


def _ssm_kernel(s_ref, bre_ref, bim_ref, cre_ref, cim_ref, lre_ref, lim_ref, y_ref,
                slab_scr, sp_scr, xre, xim, loc_re, loc_im, ent_re, ent_im, hre, him, st_re, st_im,
                *, tt, t_len):
    nchunk = tt // t_len
    n_slab = s_ref.shape[1] // LANES
    width = xre.shape[1]

    @pl.when(pl.program_id(2) == 0)
    def _():
        st_re[...] = jnp.zeros(st_re.shape, F32)
        st_im[...] = jnp.zeros(st_im.shape, F32)

    for k in range(n_slab):
        slab_scr[k] = s_ref[:, k * LANES:(k + 1) * LANES]
    for k in range(n_slab):
        for r in range(t_len):
            sp_scr[r * nchunk:(r + 1) * nchunk, k * LANES:(k + 1) * LANES] = (
                slab_scr[k, pl.ds(r, nchunk, stride=t_len), :].astype(BF16))
    sp = sp_scr[...]
    xre[...] = jnp.dot(sp, bre_ref[0], preferred_element_type=F32)
    xim[...] = jnp.dot(sp, bim_ref[0], preferred_element_type=F32)

    def advance(lr, li, hr, hi, rows, lanes):
        return (lr * hr - li * hi + xre[rows, lanes], lr * hi + li * hr + xim[rows, lanes])

    for lb in range(width // LANES):
        lanes = slice(lb * LANES, (lb + 1) * LANES)
        lr, li = lre_ref[0, :, lanes], lim_ref[0, :, lanes]
        hr, hi = xre[0:nchunk, lanes], xim[0:nchunk, lanes]
        for r in range(1, t_len):
            hr, hi = advance(lr, li, hr, hi, slice(r * nchunk, (r + 1) * nchunk), lanes)
        loc_re[:, lanes] = hr
        loc_im[:, lanes] = hi

    dr, di = lre_ref[0], lim_ref[0]
    for _ in range(t_len.bit_length() - 1):
        dr, di = dr * dr - di * di, 2.0 * dr * di

    def chunk_step(c, carry):
        sr, si = carry
        row = pl.ds(c, 1)
        ent_re[row, :] = sr
        ent_im[row, :] = si
        return (dr * sr - di * si + loc_re[row, :], dr * si + di * sr + loc_im[row, :])

    sr, si = lax.fori_loop(0, nchunk, chunk_step, (st_re[...], st_im[...]))
    st_re[...] = sr
    st_im[...] = si

    for lb in range(width // LANES):
        lanes = slice(lb * LANES, (lb + 1) * LANES)
        lr, li = lre_ref[0, :, lanes], lim_ref[0, :, lanes]
        hr, hi = ent_re[:, lanes], ent_im[:, lanes]
        for r in range(t_len):
            rows = slice(r * nchunk, (r + 1) * nchunk)
            hr, hi = advance(lr, li, hr, hi, rows, lanes)
            hre[rows, lanes] = hr.astype(BF16)
            him[rows, lanes] = hi.astype(BF16)

    yp = (jnp.dot(hre[...], cre_ref[0], preferred_element_type=F32)
          + jnp.dot(him[...], cim_ref[0], preferred_element_type=F32))
    for k in range(n_slab):
        for r in range(t_len):
            slab_scr[k, pl.ds(r, nchunk, stride=t_len), :] = yp[r * nchunk:(r + 1) * nchunk,
                                                              k * LANES:(k + 1) * LANES]
        y_ref[:, k * LANES:(k + 1) * LANES] = slab_scr[k]


def _ssm(s2d, wts, *, batch, seq):
    bre, bim, cre, cim, lre, lim = wts
    n, width = s2d.shape
    tt = min(SSM_TOKEN_TILE, seq)
    t_len = SSM_CHUNK
    assert t_len & (t_len - 1) == 0 and tt % t_len == 0 and seq % tt == 0
    tiles = seq // tt
    nchunk = tt // t_len
    col = SSM_COL_TILE
    state_w = bre.shape[2]
    tok_spec = pl.BlockSpec((tt, col), lambda b, j, i: (b * tiles + i, j))
    wspec = lambda shape: pl.BlockSpec((1,) + shape, lambda b, j, i: (j, 0, 0))
    kern = functools.partial(_ssm_kernel, tt=tt, t_len=t_len)
    return pl.pallas_call(
        kern,
        grid=(batch, width // col, tiles),
        in_specs=[tok_spec, wspec((col, state_w)), wspec((col, state_w)),
                  wspec((state_w, col)), wspec((state_w, col)), wspec((1, state_w)), wspec((1, state_w))],
        out_specs=tok_spec,
        out_shape=jax.ShapeDtypeStruct((n, width), F32),
        scratch_shapes=[pltpu.VMEM((col // LANES, tt, LANES), F32), pltpu.VMEM((tt, col), BF16),
                        pltpu.VMEM((tt, state_w), F32), pltpu.VMEM((tt, state_w), F32),
                        pltpu.VMEM((nchunk, state_w), F32), pltpu.VMEM((nchunk, state_w), F32),
                        pltpu.VMEM((nchunk, state_w), F32), pltpu.VMEM((nchunk, state_w), F32),
                        pltpu.VMEM((tt, state_w), BF16), pltpu.VMEM((tt, state_w), BF16),
                        pltpu.VMEM((1, state_w), F32), pltpu.VMEM((1, state_w), F32)],
        compiler_params=_params("parallel", "parallel", "arbitrary"),
        name="ssm",
    )(s2d, bre, bim, cre, cim, lre, lim)


def _gelu_tanh(x):
    return 0.5 * x * (1.0 + jnp.tanh(math.sqrt(2.0 / math.pi) * (x + 0.044715 * (x * x * x))))


def _mixout_kernel(attn_ref, y_ref, s_ref, x_ref, gt_ref, dskip_ref, gluw_ref, glub_ref, wout_ref,
                   g_ref, b_ref, o_ref, *, alpha, width):
    y = y_ref[...] + dskip_ref[...] * s_ref[...]
    ge = _gelu_tanh(y)
    z = jnp.dot(ge.astype(BF16), gluw_ref[...], preferred_element_type=F32) + glub_ref[...]
    ssm = (ge * jax.nn.sigmoid(z)).astype(BF16)
    h = (jnp.dot(attn_ref[...], wout_ref[0:width, :], preferred_element_type=F32)
         + jnp.dot(ssm, wout_ref[width:2 * width, :], preferred_element_type=F32))
    out = alpha * x_ref[...] + (1.0 + gt_ref[0]) * h
    o_ref[...] = _layer_norm_rows(out, g_ref[...], b_ref[...])


def _mixout(attn, y, s2d, x2d, gate, d_skip, glu_w, glu_b, w_out, ln_g, ln_b, *, seq, alpha):
    n, d = x2d.shape
    width = attn.shape[1]
    tm = min(PROJ_TOKEN_TILE, seq)
    tiles_per_seq = seq // tm
    half = pl.BlockSpec((tm, width), lambda i: (i, 0))
    full = pl.BlockSpec((tm, d), lambda i: (i, 0))
    return pl.pallas_call(
        functools.partial(_mixout_kernel, alpha=alpha, width=width),
        grid=(n // tm,),
        in_specs=[half, half, half, full,
                  pl.BlockSpec((1, 1, d), lambda i: (i // tiles_per_seq, 0, 0)),
                  _const_spec((1, width)), _const_spec(glu_w.shape), _const_spec((1, width)),
                  _const_spec(w_out.shape), _const_spec((1, d)), _const_spec((1, d))],
        out_specs=full,
        out_shape=jax.ShapeDtypeStruct((n, d), F32),
        compiler_params=_params("parallel"),
        name="mixout",
    )(attn, y, s2d, x2d, gate, d_skip.reshape(1, width), glu_w, glu_b.reshape(1, width), w_out,
      ln_g.reshape(1, d), ln_b.reshape(1, d))


def kernel(x, c, w_cond, b_cond, ffn1_w1, ffn1_w3, ffn1_w2, w_in, lambda_q1, lambda_k1, lambda_q2, lambda_k2, subln_g, ssm_a_re, ssm_a_im, ssm_log_dt, ssm_b_re, ssm_b_im, ssm_c_re, ssm_c_im, ssm_d, glu_w, glu_b, w_out, ffn2_w1, ffn2_w3, ffn2_w2, ln_g, ln_b):
    batch, seq, d = x.shape
    depth = w_cond.shape[0]
    alpha = (2 * depth) ** 0.25
    h = x.reshape(batch * seq, d)
    for i in range(depth):
        lambda_init = 0.8 - 0.6 * math.exp(-0.3 * i)
        mod = _cond(c, w_cond[i], b_cond[i]).reshape(batch, N_SUBLAYERS, 3, 1, d)
        shift, scale, gate = (lambda k: mod[:, k, 0]), (lambda k: mod[:, k, 1]), (lambda k: mod[:, k, 2])

        h = _ffn(h, shift(0), scale(0), gate(0), ffn1_w1[i].astype(BF16), ffn1_w3[i].astype(BF16),
                 ffn1_w2[i].astype(BF16), ln_g[i, 0], ln_b[i, 0], seq=seq, alpha=alpha)

        q, k1, k2, v, s = _inproj(h, shift(1), scale(1), w_in[i].astype(BF16), seq=seq)
        attn = _attention(q, k1, k2, v, lambda_q1[i], lambda_k1[i], lambda_q2[i], lambda_k2[i],
                          subln_g[i], batch=batch, seq=seq, lambda_init=lambda_init)
        wts = _ssm_weights(ssm_a_re[i], ssm_a_im[i], ssm_log_dt[i], ssm_b_re[i], ssm_b_im[i],
                           ssm_c_re[i], ssm_c_im[i])
        y = _ssm(s, wts, batch=batch, seq=seq)
        h = _mixout(attn, y, s, h, gate(1), ssm_d[i], glu_w[i].astype(BF16), glu_b[i],
                    w_out[i].astype(BF16), ln_g[i, 1], ln_b[i, 1], seq=seq, alpha=alpha)

        h = _ffn(h, shift(2), scale(2), gate(2), ffn2_w1[i].astype(BF16), ffn2_w3[i].astype(BF16),
                 ffn2_w2[i].astype(BF16), ln_g[i, 2], ln_b[i, 2], seq=seq, alpha=alpha)
    return h.reshape(batch, seq, d)
```

```python
import functools
import math

import jax
import jax.numpy as jnp
from jax import lax
from jax.experimental import pallas as pl
from jax.experimental.pallas import tpu as pltpu

F32 = jnp.float32
BF16 = jnp.bfloat16

N_SUBLAYERS = 3
DIFF_HEAD_DIM = 64
SSM_GROUP = 16
SSM_STATE = 64
LN_EPS = 1e-5
RMS_EPS = 1e-5
FFN_RES_WEIGHT = 0.5
SSM_CHUNK = 16
SSM_TOKEN_TILE = 1024
SSM_COL_TILE = 256

V7X_VMEM_BYTES = 64 * 1024 * 1024
VMEM_LIMIT = V7X_VMEM_BYTES - 4 * 1024 * 1024
LANES = 128

FFN_TOKEN_TILE = 1024
FFN_HIDDEN_TILE = 512
FFN_OUT_CHUNK = 512
ROW_CHUNK = 256
PROJ_TOKEN_TILE = 512
ATTN_TILE = 512
ATTN_ROW_STRIP = 64
LOG2E = 1.4426950408889634
COND_COL_TILE = 1024
NEG_BIG = -1e30


def _params(*sem):
    return pltpu.CompilerParams(dimension_semantics=sem, vmem_limit_bytes=VMEM_LIMIT)


def _const_spec(shape):
    nd = len(shape)
    return pl.BlockSpec(shape, lambda *_: (0,) * nd, pipeline_mode=pl.Buffered(1))


def _layer_norm_rows(y, g, b):
    mu = jnp.mean(y, axis=-1, keepdims=True)
    yc = y - mu
    var = jnp.mean(yc * yc, axis=-1, keepdims=True)
    return yc * lax.rsqrt(var + LN_EPS) * g + b


def _cond_kernel(c_ref, w_ref, b_ref, o_ref):
    c = c_ref[...]
    a = c * jax.nn.sigmoid(c)
    o_ref[...] = jnp.dot(a, w_ref[...], preferred_element_type=F32,
                         precision=lax.Precision.HIGHEST) + b_ref[...]


def _cond(c, w_cond, b_cond):
    bsz, d = c.shape
    ncol = w_cond.shape[1]
    rows = 8
    c_pad = jnp.zeros((rows, d), F32).at[:bsz].set(c)
    tn = COND_COL_TILE
    out = pl.pallas_call(
        _cond_kernel,
        grid=(ncol // tn,),
        in_specs=[pl.BlockSpec((rows, d), lambda j: (0, 0)),
                  pl.BlockSpec((d, tn), lambda j: (0, j)),
                  pl.BlockSpec((1, tn), lambda j: (0, j))],
        out_specs=pl.BlockSpec((rows, tn), lambda j: (0, j)),
        out_shape=jax.ShapeDtypeStruct((rows, ncol), F32),
        compiler_params=_params("arbitrary"),
        name="cond",
    )(c_pad, w_cond, b_cond.reshape(1, ncol))
    return out[:bsz]


def _ffn_kernel(x_ref, sh_ref, sc_ref, gt_ref, w1_ref, w3_ref, w2_ref, g_ref, b_ref,
                o_ref, u_ref, *, alpha, n_hidden_tiles, tm, d):
    j = pl.program_id(1)

    @pl.when(j == 0)
    def _():
        for r in range(0, tm, ROW_CHUNK):
            rows = pl.ds(r, ROW_CHUNK)
            u_ref[rows, :] = (x_ref[rows, :] * (1.0 + sc_ref[0]) + sh_ref[0]).astype(BF16)

    u = u_ref[...]
    h1 = jnp.dot(u, w1_ref[...], preferred_element_type=F32)
    h3 = jnp.dot(u, w3_ref[...], preferred_element_type=F32)
    a = (h1 * jax.nn.sigmoid(h1) * h3).astype(BF16)

    @pl.when(j == 0)
    def _():
        for n in range(0, d, FFN_OUT_CHUNK):
            cols = pl.ds(n, FFN_OUT_CHUNK)
            o_ref[:, cols] = jnp.dot(a, w2_ref[:, cols], preferred_element_type=F32)

    @pl.when(j > 0)
    def _():
        for n in range(0, d, FFN_OUT_CHUNK):
            cols = pl.ds(n, FFN_OUT_CHUNK)
            o_ref[:, cols] += jnp.dot(a, w2_ref[:, cols], preferred_element_type=F32)

    @pl.when(j == n_hidden_tiles - 1)
    def _():
        gate = FFN_RES_WEIGHT * (1.0 + gt_ref[0])
        for r in range(0, tm, ROW_CHUNK):
            rows = pl.ds(r, ROW_CHUNK)
            y = alpha * x_ref[rows, :] + gate * o_ref[rows, :]
            o_ref[rows, :] = _layer_norm_rows(y, g_ref[...], b_ref[...])


def _ffn(x2d, shift, scale, gate, w1, w3, w2, ln_g, ln_b, *, seq, alpha):
    n, d = x2d.shape
    f = w1.shape[1]
    tm = min(FFN_TOKEN_TILE, seq)
    tf = FFN_HIDDEN_TILE
    tiles_per_seq = seq // tm
    nj = f // tf
    mod_spec = pl.BlockSpec((1, 1, d), lambda i, j: (i // tiles_per_seq, 0, 0))
    vec_spec = pl.BlockSpec((1, d), lambda i, j: (0, 0))
    kern = functools.partial(_ffn_kernel, alpha=alpha, n_hidden_tiles=nj, tm=tm, d=d)
    return pl.pallas_call(
        kern,
        grid=(n // tm, nj),
        in_specs=[pl.BlockSpec((tm, d), lambda i, j: (i, 0)),
                  mod_spec, mod_spec, mod_spec,
                  pl.BlockSpec((d, tf), lambda i, j: (0, j)),
                  pl.BlockSpec((d, tf), lambda i, j: (0, j)),
                  pl.BlockSpec((tf, d), lambda i, j: (j, 0)),
                  vec_spec, vec_spec],
        out_specs=pl.BlockSpec((tm, d), lambda i, j: (i, 0)),
        out_shape=jax.ShapeDtypeStruct((n, d), F32),
        scratch_shapes=[pltpu.VMEM((tm, d), BF16)],
        compiler_params=_params("parallel", "arbitrary"),
        name="ffn",
    )(x2d, shift, scale, gate, w1, w3, w2, ln_g.reshape(1, d), ln_b.reshape(1, d))


def _inproj_kernel(x_ref, sh_ref, sc_ref, w_ref, q_ref, k1_ref, k2_ref, v_ref, s_ref, *, width):
    u = (x_ref[...] * (1.0 + sc_ref[0]) + sh_ref[0]).astype(BF16)
    qk_scale = LOG2E / math.sqrt(DIFF_HEAD_DIM)
    q = jnp.dot(u, w_ref[:, 0:width], preferred_element_type=F32)
    q_ref[...] = (q * qk_scale).astype(BF16)
    k = jnp.dot(u, w_ref[:, width:2 * width], preferred_element_type=F32)
    lane = lax.broadcasted_iota(jnp.int32, k.shape, 1)
    first = (lane & DIFF_HEAD_DIM) == 0
    k1_ref[...] = jnp.where(first, k, 0.0).astype(BF16)
    k2_ref[...] = jnp.where(first, 0.0, k).astype(BF16)
    v_ref[...] = jnp.dot(u, w_ref[:, 2 * width:3 * width], preferred_element_type=F32).astype(BF16)
    s_ref[...] = jnp.dot(u, w_ref[:, 3 * width:4 * width], preferred_element_type=F32)


def _inproj(x2d, shift, scale, w_in, *, seq):
    n, d = x2d.shape
    width = w_in.shape[1] // 4
    tm = min(PROJ_TOKEN_TILE, seq)
    tiles_per_seq = seq // tm
    mod_spec = pl.BlockSpec((1, 1, d), lambda i: (i // tiles_per_seq, 0, 0))
    out_spec = pl.BlockSpec((tm, width), lambda i: (i, 0))
    bf = jax.ShapeDtypeStruct((n, width), BF16)
    return pl.pallas_call(
        functools.partial(_inproj_kernel, width=width),
        grid=(n // tm,),
        in_specs=[pl.BlockSpec((tm, d), lambda i: (i, 0)), mod_spec, mod_spec,
                  _const_spec(w_in.shape)],
        out_specs=[out_spec] * 5,
        out_shape=[bf, bf, bf, bf, jax.ShapeDtypeStruct((n, width), F32)],
        compiler_params=_params("parallel"),
        name="inproj",
    )(x2d, shift, scale, w_in)


def _attn_kernel(q_ref, k1_ref, k2_ref, v_ref, lq1_ref, lk1_ref, lq2_ref, lk2_ref, g_ref,
                 o_ref, sa_scr, sb_scr, pma_scr, pmb_scr, p_scr, m_scr, l_scr, acc_scr,
                 *, tq, tk, lambda_init):
    qi = pl.program_id(2)
    nt = (((1,), (1,)), ((), ()))
    k_refs = (k1_ref, k2_ref)
    m_scr[...] = jnp.full(m_scr.shape, NEG_BIG, F32)
    l_scr[...] = jnp.zeros(l_scr.shape, F32)
    acc_scr[...] = jnp.zeros(acc_scr.shape, F32)
    ones = jnp.ones((tk, LANES), BF16)

    def tile_max(s):
        pm = s[:, 0:LANES]
        for t in range(1, tk // LANES):
            pm = jnp.maximum(pm, s[:, t * LANES:(t + 1) * LANES])
        return pm

    def scores(kb, s_scr, pm_scr):
        rows_k = pl.ds(pl.multiple_of(kb * tk, tk), tk)
        q = q_ref[...]
        for mp in range(2):
            s = lax.dot_general(q, k_refs[mp][rows_k, :], nt, preferred_element_type=F32)
            s_scr[mp] = s
            pm_scr[mp] = tile_max(s)

    def consume(kb, s_scr, pm_scr, masked):
        rows_k = pl.ds(pl.multiple_of(kb * tk, tk), tk)
        v_ext = jnp.concatenate([v_ref[rows_k, :], ones], axis=1)
        for mp in range(2):
            if masked:
                r = lax.broadcasted_iota(jnp.int32, (tq, tk), 0)
                c = lax.broadcasted_iota(jnp.int32, (tq, tk), 1)
                s = jnp.where(c <= r, s_scr[mp], -jnp.inf)
                s_scr[mp] = s
                pm = tile_max(s)
            else:
                pm = pm_scr[mp]
            m_old = m_scr[mp]
            m_new = jnp.maximum(m_old, jnp.broadcast_to(jnp.max(pm, axis=-1, keepdims=True), (tq, LANES)))
            m_scr[mp] = m_new
            corr = jnp.exp2(m_old - m_new)
            for r0 in range(0, tq, ATTN_ROW_STRIP):
                rs = pl.ds(r0, ATTN_ROW_STRIP)
                p = jnp.exp2(s_scr[mp, rs, :] - jnp.concatenate([m_scr[mp, rs, :]] * (tk // LANES), axis=1))
                p_scr[mp, rs, :] = p.astype(BF16)
            res = jnp.dot(p_scr[mp], v_ext, preferred_element_type=F32)
            acc_scr[mp] = corr * acc_scr[mp] + res[:, :LANES]
            l_scr[mp] = corr * l_scr[mp] + res[:, LANES:]

    scores(0, sa_scr, pma_scr)

    def body(j, c):
        scores(2 * j + 1, sb_scr, pmb_scr)
        consume(2 * j, sa_scr, pma_scr, False)
        scores(2 * j + 2, sa_scr, pma_scr)
        consume(2 * j + 1, sb_scr, pmb_scr, False)
        return c
    lax.fori_loop(0, qi // 2, body, 0)

    @pl.when(qi % 2 == 0)
    def _():
        consume(qi, sa_scr, pma_scr, True)

    @pl.when(qi % 2 == 1)
    def _():
        scores(qi, sb_scr, pmb_scr)
        consume(qi - 1, sa_scr, pma_scr, False)
        consume(qi, sb_scr, pmb_scr, True)

    lam = (jnp.exp(jnp.sum(lq1_ref[...] * lk1_ref[...], axis=-1, keepdims=True))
           - jnp.exp(jnp.sum(lq2_ref[...] * lk2_ref[...], axis=-1, keepdims=True)) + lambda_init)
    o = acc_scr[0] / l_scr[0] - lam * (acc_scr[1] / l_scr[1])
    o = o * lax.rsqrt(jnp.mean(o * o, axis=-1, keepdims=True) + RMS_EPS)
    o_ref[...] = (o * g_ref[...] * (1.0 - lambda_init)).astype(BF16)


def _attention(q, k1, k2, v, lq1, lk1, lq2, lk2, subln_g, *, batch, seq, lambda_init):
    n, width = q.shape
    heads = width // LANES
    tq = min(ATTN_TILE, seq)
    tk = tq
    nq = seq // tq
    kv_spec = pl.BlockSpec((seq, LANES), lambda b, h, i: (b, h))
    lam_spec = pl.BlockSpec((1, DIFF_HEAD_DIM), lambda b, h, i: (0, 0))
    q_spec = pl.BlockSpec((tq, LANES), lambda b, h, i: (b * nq + i, h))
    kern = functools.partial(_attn_kernel, tq=tq, tk=tk, lambda_init=lambda_init)
    row = lambda a: a.reshape(1, -1).astype(F32)
    score_buf = pltpu.VMEM((2, tq, tk), F32)
    stat_buf = pltpu.VMEM((2, tq, LANES), F32)
    return pl.pallas_call(
        kern,
        grid=(batch, heads, nq),
        in_specs=[q_spec, kv_spec, kv_spec, kv_spec, lam_spec, lam_spec, lam_spec, lam_spec,
                  pl.BlockSpec((1, LANES), lambda b, h, i: (0, 0))],
        out_specs=q_spec,
        out_shape=jax.ShapeDtypeStruct((n, width), BF16),
        scratch_shapes=[score_buf, score_buf, stat_buf, stat_buf, pltpu.VMEM((2, tq, tk), BF16),
                        stat_buf, stat_buf, stat_buf],
        compiler_params=_params("parallel", "parallel", "arbitrary"),
        name="attn",
    )(q, k1, k2, v, row(lq1), row(lk1), row(lq2), row(lk2), row(subln_g))


def _ssm_weights(a_re, a_im, log_dt, b_re, b_im, c_re, c_im):
    g_n, p_n, h_n = b_re.shape
    gpb = SSM_COL_TILE // h_n
    nb = g_n // gpb
    dt = jnp.exp(log_dt)[:, None]
    mag = jnp.exp(a_re * dt)
    lbar_re = mag * jnp.cos(a_im * dt)
    lbar_im = mag * jnp.sin(a_im * dt)
    nr, ni = lbar_re - 1.0, lbar_im
    den = jnp.square(a_re) + jnp.square(a_im)
    coef_re = ((nr * a_re + ni * a_im) / den)[..., None]
    coef_im = ((ni * a_re - nr * a_im) / den)[..., None]
    bb_re = coef_re * b_re - coef_im * b_im
    bb_im = coef_re * b_im + coef_im * b_re
    eye = jnp.eye(gpb, dtype=F32)

    def in_diag(bb):
        m = bb.reshape(nb, gpb, p_n, h_n).transpose(0, 1, 3, 2)
        return (m[:, :, :, None, :] * eye[None, :, None, :, None]).reshape(nb, gpb * h_n, gpb * p_n)

    def out_diag(c):
        m = c.reshape(nb, gpb, h_n, p_n).transpose(0, 1, 3, 2)
        return (m[:, :, :, None, :] * eye[None, :, None, :, None]).reshape(nb, gpb * p_n, gpb * h_n)

    lam = lambda v: v.reshape(nb, 1, gpb * p_n)
    return (in_diag(bb_re).astype(BF16), in_diag(bb_im).astype(BF16),
            out_diag(c_re).astype(BF16), out_diag(-c_im).astype(BF16), lam(lbar_re), lam(lbar_im))


def _ssm_kernel(s_ref, bre_ref, bim_ref, cre_ref, cim_ref, lre_ref, lim_ref, y_ref,
                slab_scr, sp_scr, xre, xim, loc_re, loc_im, ent_re, ent_im, hre, him, st_re, st_im,
                *, tt, t_len):
    nchunk = tt // t_len
    n_slab = s_ref.shape[1] // LANES
    width = xre.shape[1]

    @pl.when(pl.program_id(2) == 0)
    def _():
        st_re[...] = jnp.zeros(st_re.shape, F32)
        st_im[...] = jnp.zeros(st_im.shape, F32)

    for k in range(n_slab):
        slab_scr[k] = s_ref[:, k * LANES:(k + 1) * LANES]
    for k in range(n_slab):
        for r in range(t_len):
            sp_scr[r * nchunk:(r + 1) * nchunk, k * LANES:(k + 1) * LANES] = (
                slab_scr[k, pl.ds(r, nchunk, stride=t_len), :].astype(BF16))
    sp = sp_scr[...]
    xre[...] = jnp.dot(sp, bre_ref[0], preferred_element_type=F32)
    xim[...] = jnp.dot(sp, bim_ref[0], preferred_element_type=F32)

    def advance(lr, li, hr, hi, rows, lanes):
        return (lr * hr - li * hi + xre[rows, lanes], lr * hi + li * hr + xim[rows, lanes])

    for lb in range(width // LANES):
        lanes = slice(lb * LANES, (lb + 1) * LANES)
        lr, li = lre_ref[0, :, lanes], lim_ref[0, :, lanes]
        hr, hi = xre[0:nchunk, lanes], xim[0:nchunk, lanes]
        for r in range(1, t_len):
            hr, hi = advance(lr, li, hr, hi, slice(r * nchunk, (r + 1) * nchunk), lanes)
        loc_re[:, lanes] = hr
        loc_im[:, lanes] = hi

    dr, di = lre_ref[0], lim_ref[0]
    for _ in range(t_len.bit_length() - 1):
        dr, di = dr * dr - di * di, 2.0 * dr * di

    def chunk_step(c, carry):
        sr, si = carry
        row = pl.ds(c, 1)
        ent_re[row, :] = sr
        ent_im[row, :] = si
        return (dr * sr - di * si + loc_re[row, :], dr * si + di * sr + loc_im[row, :])

    sr, si = lax.fori_loop(0, nchunk, chunk_step, (st_re[...], st_im[...]))
    st_re[...] = sr
    st_im[...] = si

    for lb in range(width // LANES):
        lanes = slice(lb * LANES, (lb + 1) * LANES)
        lr, li = lre_ref[0, :, lanes], lim_ref[0, :, lanes]
        hr, hi = ent_re[:, lanes], ent_im[:, lanes]
        for r in range(t_len):
            rows = slice(r * nchunk, (r + 1) * nchunk)
            hr, hi = advance(lr, li, hr, hi, rows, lanes)
            hre[rows, lanes] = hr.astype(BF16)
            him[rows, lanes] = hi.astype(BF16)

    yp = (jnp.dot(hre[...], cre_ref[0], preferred_element_type=F32)
          + jnp.dot(him[...], cim_ref[0], preferred_element_type=F32))
    for k in range(n_slab):
        for r in range(t_len):
            slab_scr[k, pl.ds(r, nchunk, stride=t_len), :] = yp[r * nchunk:(r + 1) * nchunk,
                                                              k * LANES:(k + 1) * LANES]
        y_ref[:, k * LANES:(k + 1) * LANES] = slab_scr[k]


def _ssm(s2d, wts, *, batch, seq):
    bre, bim, cre, cim, lre, lim = wts
    n, width = s2d.shape
    tt = min(SSM_TOKEN_TILE, seq)
    t_len = SSM_CHUNK
    assert t_len & (t_len - 1) == 0 and tt % t_len == 0 and seq % tt == 0
    tiles = seq // tt
    nchunk = tt // t_len
    col = SSM_COL_TILE
    state_w = bre.shape[2]
    tok_spec = pl.BlockSpec((tt, col), lambda b, j, i: (b * tiles + i, j))
    wspec = lambda shape: pl.BlockSpec((1,) + shape, lambda b, j, i: (j, 0, 0))
    kern = functools.partial(_ssm_kernel, tt=tt, t_len=t_len)
    return pl.pallas_call(
        kern,
        grid=(batch, width // col, tiles),
        in_specs=[tok_spec, wspec((col, state_w)), wspec((col, state_w)),
                  wspec((state_w, col)), wspec((state_w, col)), wspec((1, state_w)), wspec((1, state_w))],
        out_specs=tok_spec,
        out_shape=jax.ShapeDtypeStruct((n, width), F32),
        scratch_shapes=[pltpu.VMEM((col // LANES, tt, LANES), F32), pltpu.VMEM((tt, col), BF16),
                        pltpu.VMEM((tt, state_w), F32), pltpu.VMEM((tt, state_w), F32),
                        pltpu.VMEM((nchunk, state_w), F32), pltpu.VMEM((nchunk, state_w), F32),
                        pltpu.VMEM((nchunk, state_w), F32), pltpu.VMEM((nchunk, state_w), F32),
                        pltpu.VMEM((tt, state_w), BF16), pltpu.VMEM((tt, state_w), BF16),
                        pltpu.VMEM((1, state_w), F32), pltpu.VMEM((1, state_w), F32)],
        compiler_params=_params("parallel", "parallel", "arbitrary"),
        name="ssm",
    )(s2d, bre, bim, cre, cim, lre, lim)


def _gelu_tanh(x):
    return 0.5 * x * (1.0 + jnp.tanh(math.sqrt(2.0 / math.pi) * (x + 0.044715 * (x * x * x))))


def _mixout_kernel(attn_ref, y_ref, s_ref, x_ref, gt_ref, dskip_ref, gluw_ref, glub_ref, wout_ref,
                   g_ref, b_ref, o_ref, *, alpha, width):
    y = y_ref[...] + dskip_ref[...] * s_ref[...]
    ge = _gelu_tanh(y)
    z = jnp.dot(ge.astype(BF16), gluw_ref[...], preferred_element_type=F32) + glub_ref[...]
    ssm = (ge * jax.nn.sigmoid(z)).astype(BF16)
    h = (jnp.dot(attn_ref[...], wout_ref[0:width, :], preferred_element_type=F32)
         + jnp.dot(ssm, wout_ref[width:2 * width, :], preferred_element_type=F32))
    out = alpha * x_ref[...] + (1.0 + gt_ref[0]) * h
    o_ref[...] = _layer_norm_rows(out, g_ref[...], b_ref[...])


def _mixout(attn, y, s2d, x2d, gate, d_skip, glu_w, glu_b, w_out, ln_g, ln_b, *, seq, alpha):
    n, d = x2d.shape
    width = attn.shape[1]
    tm = min(PROJ_TOKEN_TILE, seq)
    tiles_per_seq = seq // tm
    half = pl.BlockSpec((tm, width), lambda i: (i, 0))
    full = pl.BlockSpec((tm, d), lambda i: (i, 0))
    return pl.pallas_call(
        functools.partial(_mixout_kernel, alpha=alpha, width=width),
        grid=(n // tm,),
        in_specs=[half, half, half, full,
                  pl.BlockSpec((1, 1, d), lambda i: (i // tiles_per_seq, 0, 0)),
                  _const_spec((1, width)), _const_spec(glu_w.shape), _const_spec((1, width)),
                  _const_spec(w_out.shape), _const_spec((1, d)), _const_spec((1, d))],
        out_specs=full,
        out_shape=jax.ShapeDtypeStruct((n, d), F32),
        compiler_params=_params("parallel"),
        name="mixout",
    )(attn, y, s2d, x2d, gate, d_skip.reshape(1, width), glu_w, glu_b.reshape(1, width), w_out,
      ln_g.reshape(1, d), ln_b.reshape(1, d))


def kernel(x, c, w_cond, b_cond, ffn1_w1, ffn1_w3, ffn1_w2, w_in, lambda_q1, lambda_k1, lambda_q2, lambda_k2, subln_g, ssm_a_re, ssm_a_im, ssm_log_dt, ssm_b_re, ssm_b_im, ssm_c_re, ssm_c_im, ssm_d, glu_w, glu_b, w_out, ffn2_w1, ffn2_w3, ffn2_w2, ln_g, ln_b):
    batch, seq, d = x.shape
    depth = w_cond.shape[0]
    alpha = (2 * depth) ** 0.25
    h = x.reshape(batch * seq, d)
    for i in range(depth):
        lambda_init = 0.8 - 0.6 * math.exp(-0.3 * i)
        mod = _cond(c, w_cond[i], b_cond[i]).reshape(batch, N_SUBLAYERS, 3, 1, d)
        shift, scale, gate = (lambda k: mod[:, k, 0]), (lambda k: mod[:, k, 1]), (lambda k: mod[:, k, 2])

        h = _ffn(h, shift(0), scale(0), gate(0), ffn1_w1[i].astype(BF16), ffn1_w3[i].astype(BF16),
                 ffn1_w2[i].astype(BF16), ln_g[i, 0], ln_b[i, 0], seq=seq, alpha=alpha)

        q, k1, k2, v, s = _inproj(h, shift(1), scale(1), w_in[i].astype(BF16), seq=seq)
        attn = _attention(q, k1, k2, v, lambda_q1[i], lambda_k1[i], lambda_q2[i], lambda_k2[i],
                          subln_g[i], batch=batch, seq=seq, lambda_init=lambda_init)
        wts = _ssm_weights(ssm_a_re[i], ssm_a_im[i], ssm_log_dt[i], ssm_b_re[i], ssm_b_im[i],
                           ssm_c_re[i], ssm_c_im[i])
        y = _ssm(s, wts, batch=batch, seq=seq)
        h = _mixout(attn, y, s, h, gate(1), ssm_d[i], glu_w[i].astype(BF16), glu_b[i],
                    w_out[i].astype(BF16), ln_g[i, 1], ln_b[i, 1], seq=seq, alpha=alpha)

        h = _ffn(h, shift(2), scale(2), gate(2), ffn2_w1[i].astype(BF16), ffn2_w3[i].astype(BF16),
                 ffn2_w2[i].astype(BF16), ln_g[i, 2], ln_b[i, 2], seq=seq, alpha=alpha)
    return h.reshape(batch, seq, d)
```

```python
import functools
import math

import jax
import jax.numpy as jnp
from jax import lax
from jax.experimental import pallas as pl
from jax.experimental.pallas import tpu as pltpu

F32 = jnp.float32
BF16 = jnp.bfloat16

N_SUBLAYERS = 3
DIFF_HEAD_DIM = 64
SSM_GROUP = 16
SSM_STATE = 64
LN_EPS = 1e-5
RMS_EPS = 1e-5
FFN_RES_WEIGHT = 0.5
SSM_CHUNK = 16
SSM_TOKEN_TILE = 1024
SSM_COL_TILE = 256

V7X_VMEM_BYTES = 64 * 1024 * 1024
VMEM_LIMIT = V7X_VMEM_BYTES - 4 * 1024 * 1024
LANES = 128

FFN_TOKEN_TILE = 1024
FFN_HIDDEN_TILE = 512
FFN_OUT_CHUNK = 512
ROW_CHUNK = 256
PROJ_TOKEN_TILE = 512
ATTN_TILE = 512
ATTN_ROW_STRIP = 64
LOG2E = 1.4426950408889634
COND_COL_TILE = 1024
NEG_BIG = -1e30


def _params(*sem):
    return pltpu.CompilerParams(dimension_semantics=sem, vmem_limit_bytes=VMEM_LIMIT)


def _const_spec(shape):
    nd = len(shape)
    return pl.BlockSpec(shape, lambda *_: (0,) * nd, pipeline_mode=pl.Buffered(1))


def _layer_norm_rows(y, g, b):
    mu = jnp.mean(y, axis=-1, keepdims=True)
    yc = y - mu
    var = jnp.mean(yc * yc, axis=-1, keepdims=True)
    return yc * lax.rsqrt(var + LN_EPS) * g + b


def _cond_kernel(c_ref, w_ref, b_ref, o_ref):
    c = c_ref[...]
    a = c * jax.nn.sigmoid(c)
    o_ref[...] = jnp.dot(a, w_ref[...], preferred_element_type=F32,
                         precision=lax.Precision.HIGHEST) + b_ref[...]


def _cond(c, w_cond, b_cond):
    bsz, d = c.shape
    ncol = w_cond.shape[1]
    rows = 8
    c_pad = jnp.zeros((rows, d), F32).at[:bsz].set(c)
    tn = COND_COL_TILE
    out = pl.pallas_call(
        _cond_kernel,
        grid=(ncol // tn,),
        in_specs=[pl.BlockSpec((rows, d), lambda j: (0, 0)),
                  pl.BlockSpec((d, tn), lambda j: (0, j)),
                  pl.BlockSpec((1, tn), lambda j: (0, j))],
        out_specs=pl.BlockSpec((rows, tn), lambda j: (0, j)),
        out_shape=jax.ShapeDtypeStruct((rows, ncol), F32),
        compiler_params=_params("arbitrary"),
        name="cond",
    )(c_pad, w_cond, b_cond.reshape(1, ncol))
    return out[:bsz]


def _ffn_kernel(x_ref, sh_ref, sc_ref, gt_ref, w1_ref, w3_ref, w2_ref, g_ref, b_ref,
                o_ref, u_ref, *, alpha, n_hidden_tiles, tm, d):
    j = pl.program_id(1)

    @pl.when(j == 0)
    def _():
        for r in range(0, tm, ROW_CHUNK):
            rows = pl.ds(r, ROW_CHUNK)
            u_ref[rows, :] = (x_ref[rows, :] * (1.0 + sc_ref[0]) + sh_ref[0]).astype(BF16)

    u = u_ref[...]
    h1 = jnp.dot(u, w1_ref[...], preferred_element_type=F32)
    h3 = jnp.dot(u, w3_ref[...], preferred_element_type=F32)
    a = (h1 * jax.nn.sigmoid(h1) * h3).astype(BF16)

    @pl.when(j == 0)
    def _():
        for n in range(0, d, FFN_OUT_CHUNK):
            cols = pl.ds(n, FFN_OUT_CHUNK)
            o_ref[:, cols] = jnp.dot(a, w2_ref[:, cols], preferred_element_type=F32)

    @pl.when(j > 0)
    def _():
        for n in range(0, d, FFN_OUT_CHUNK):
            cols = pl.ds(n, FFN_OUT_CHUNK)
            o_ref[:, cols] += jnp.dot(a, w2_ref[:, cols], preferred_element_type=F32)

    @pl.when(j == n_hidden_tiles - 1)
    def _():
        gate = FFN_RES_WEIGHT * (1.0 + gt_ref[0])
        for r in range(0, tm, ROW_CHUNK):
            rows = pl.ds(r, ROW_CHUNK)
            y = alpha * x_ref[rows, :] + gate * o_ref[rows, :]
            o_ref[rows, :] = _layer_norm_rows(y, g_ref[...], b_ref[...])


def _ffn(x2d, shift, scale, gate, w1, w3, w2, ln_g, ln_b, *, seq, alpha):
    n, d = x2d.shape
    f = w1.shape[1]
    tm = min(FFN_TOKEN_TILE, seq)
    tf = FFN_HIDDEN_TILE
    tiles_per_seq = seq // tm
    nj = f // tf
    mod_spec = pl.BlockSpec((1, 1, d), lambda i, j: (i // tiles_per_seq, 0, 0))
    vec_spec = pl.BlockSpec((1, d), lambda i, j: (0, 0))
    kern = functools.partial(_ffn_kernel, alpha=alpha, n_hidden_tiles=nj, tm=tm, d=d)
    return pl.pallas_call(
        kern,
        grid=(n // tm, nj),
        in_specs=[pl.BlockSpec((tm, d), lambda i, j: (i, 0)),
                  mod_spec, mod_spec, mod_spec,
                  pl.BlockSpec((d, tf), lambda i, j: (0, j)),
                  pl.BlockSpec((d, tf), lambda i, j: (0, j)),
                  pl.BlockSpec((tf, d), lambda i, j: (j, 0)),
                  vec_spec, vec_spec],
        out_specs=pl.BlockSpec((tm, d), lambda i, j: (i, 0)),
        out_shape=jax.ShapeDtypeStruct((n, d), F32),
        scratch_shapes=[pltpu.VMEM((tm, d), BF16)],
        compiler_params=_params("parallel", "arbitrary"),
        name="ffn",
    )(x2d, shift, scale, gate, w1, w3, w2, ln_g.reshape(1, d), ln_b.reshape(1, d))


def _inproj_kernel(x_ref, sh_ref, sc_ref, w_ref, q_ref, k1_ref, k2_ref, v_ref, s_ref, *, width):
    u = (x_ref[...] * (1.0 + sc_ref[0]) + sh_ref[0]).astype(BF16)
    qk_scale = LOG2E / math.sqrt(DIFF_HEAD_DIM)
    q = jnp.dot(u, w_ref[:, 0:width], preferred_element_type=F32)
    q_ref[...] = (q * qk_scale).astype(BF16)
    k = jnp.dot(u, w_ref[:, width:2 * width], preferred_element_type=F32)
    lane = lax.broadcasted_iota(jnp.int32, k.shape, 1)
    first = (lane & DIFF_HEAD_DIM) == 0
    k1_ref[...] = jnp.where(first, k, 0.0).astype(BF16)
    k2_ref[...] = jnp.where(first, 0.0, k).astype(BF16)
    v_ref[...] = jnp.dot(u, w_ref[:, 2 * width:3 * width], preferred_element_type=F32).astype(BF16)
    s_ref[...] = jnp.dot(u, w_ref[:, 3 * width:4 * width], preferred_element_type=F32)


def _inproj(x2d, shift, scale, w_in, *, seq):
    n, d = x2d.shape
    width = w_in.shape[1] // 4
    tm = min(PROJ_TOKEN_TILE, seq)
    tiles_per_seq = seq // tm
    mod_spec = pl.BlockSpec((1, 1, d), lambda i: (i // tiles_per_seq, 0, 0))
    out_spec = pl.BlockSpec((tm, width), lambda i: (i, 0))
    bf = jax.ShapeDtypeStruct((n, width), BF16)
    return pl.pallas_call(
        functools.partial(_inproj_kernel, width=width),
        grid=(n // tm,),
        in_specs=[pl.BlockSpec((tm, d), lambda i: (i, 0)), mod_spec, mod_spec,
                  _const_spec(w_in.shape)],
        out_specs=[out_spec] * 5,
        out_shape=[bf, bf, bf, bf, jax.ShapeDtypeStruct((n, width), F32)],
        compiler_params=_params("parallel"),
        name="inproj",
    )(x2d, shift, scale, w_in)


def _attn_kernel(q_ref, k1_ref, k2_ref, v_ref, lq1_ref, lk1_ref, lq2_ref, lk2_ref, g_ref,
                 o_ref, vt_scr, sa_scr, sb_scr, pma_scr, pmb_scr, p_scr, m_scr, acc_scr,
                 *, tq, tk, lambda_init):
    qi = pl.program_id(2)
    nt = (((1,), (1,)), ((), ()))
    k_refs = (k1_ref, k2_ref)
    n_kv = vt_scr.shape[0]
    v_rows = vt_scr.shape[1]

    @pl.when(qi == 0)
    def _():
        for kb in range(n_kv):
            vt = jnp.transpose(v_ref[kb * tk:(kb + 1) * tk, :].astype(F32))
            vt_scr[kb, 0:LANES, :] = vt.astype(BF16)
            vt_scr[kb, LANES:v_rows, :] = jnp.ones((v_rows - LANES, tk), BF16)

    m_scr[...] = jnp.full(m_scr.shape, NEG_BIG, F32)
    acc_scr[...] = jnp.zeros(acc_scr.shape, F32)

    def row_max(s):
        return jnp.max(s.reshape(tk // 8, 8, tq), axis=0)

    def scores(kb, s_scr, pm_scr):
        rows_k = pl.ds(pl.multiple_of(kb * tk, tk), tk)
        q = q_ref[...]
        for mp in range(2):
            s = lax.dot_general(k_refs[mp][rows_k, :], q, nt, preferred_element_type=F32)
            s_scr[mp] = s
            pm_scr[mp] = row_max(s)

    def consume(kb, s_scr, pm_scr, masked):
        vt = vt_scr[kb]
        for mp in range(2):
            if masked:
                r = lax.broadcasted_iota(jnp.int32, (tk, tq), 0)
                c = lax.broadcasted_iota(jnp.int32, (tk, tq), 1)
                s = jnp.where(r <= c, s_scr[mp], -jnp.inf)
                s_scr[mp] = s
                pm = row_max(s)
            else:
                pm = pm_scr[mp]
            m_old = m_scr[mp]
            m_new = jnp.maximum(m_old, jnp.broadcast_to(jnp.max(pm, axis=0, keepdims=True), (8, tq)))
            m_scr[mp] = m_new
            corr = jnp.exp2(m_old - m_new)
            m_rows = jnp.concatenate([m_new] * (ATTN_ROW_STRIP // 8), axis=0)
            for r0 in range(0, tk, ATTN_ROW_STRIP):
                rs = pl.ds(r0, ATTN_ROW_STRIP)
                p_scr[mp, rs, :] = jnp.exp2(s_scr[mp, rs, :] - m_rows).astype(BF16)
            res = jnp.dot(vt, p_scr[mp], preferred_element_type=F32)
            acc_scr[mp] = jnp.concatenate([corr] * (v_rows // 8), axis=0) * acc_scr[mp] + res

    scores(0, sa_scr, pma_scr)

    def pair(kb):
        scores(kb + 1, sb_scr, pmb_scr)
        consume(kb, sa_scr, pma_scr, False)
        scores(kb + 2, sa_scr, pma_scr)
        consume(kb + 1, sb_scr, pmb_scr, False)

    def body(j, c):
        pair(4 * j)
        pair(4 * j + 2)
        return c
    lax.fori_loop(0, qi // 4, body, 0)

    @pl.when(qi % 4 >= 2)
    def _():
        pair((qi // 4) * 4)

    @pl.when(qi % 2 == 0)
    def _():
        consume(qi, sa_scr, pma_scr, True)

    @pl.when(qi % 2 == 1)
    def _():
        scores(qi, sb_scr, pmb_scr)
        consume(qi - 1, sa_scr, pma_scr, False)
        consume(qi, sb_scr, pmb_scr, True)

    lam = (jnp.exp(jnp.sum(lq1_ref[...] * lk1_ref[...], axis=-1, keepdims=True))
           - jnp.exp(jnp.sum(lq2_ref[...] * lk2_ref[...], axis=-1, keepdims=True)) + lambda_init)

    def normalised(mp):
        denom = jnp.concatenate([acc_scr[mp, LANES:LANES + 8, :]] * (LANES // 8), axis=0)
        return acc_scr[mp, 0:LANES, :] / denom

    o = jnp.transpose(normalised(0) - lam * normalised(1))
    o = o * lax.rsqrt(jnp.mean(o * o, axis=-1, keepdims=True) + RMS_EPS)
    o_ref[...] = (o * g_ref[...] * (1.0 - lambda_init)).astype(BF16)


def _attention(q, k1, k2, v, lq1, lk1, lq2, lk2, subln_g, *, batch, seq, lambda_init):
    n, width = q.shape
    heads = width // LANES
    tq = min(ATTN_TILE, seq)
    tk = tq
    nq = seq // tq
    v_rows = LANES + 16
    kv_spec = pl.BlockSpec((seq, LANES), lambda b, h, i: (b, h))
    lam_spec = pl.BlockSpec((1, DIFF_HEAD_DIM), lambda b, h, i: (0, 0))
    q_spec = pl.BlockSpec((tq, LANES), lambda b, h, i: (b * nq + i, h))
    kern = functools.partial(_attn_kernel, tq=tq, tk=tk, lambda_init=lambda_init)
    row = lambda a: a.reshape(1, -1).astype(F32)
    score_buf = pltpu.VMEM((2, tk, tq), F32)
    stat_buf = pltpu.VMEM((2, 8, tq), F32)
    return pl.pallas_call(
        kern,
        grid=(batch, heads, nq),
        in_specs=[q_spec, kv_spec, kv_spec, kv_spec, lam_spec, lam_spec, lam_spec, lam_spec,
                  pl.BlockSpec((1, LANES), lambda b, h, i: (0, 0))],
        out_specs=q_spec,
        out_shape=jax.ShapeDtypeStruct((n, width), BF16),
        scratch_shapes=[pltpu.VMEM((seq // tk, v_rows, tk), BF16), score_buf, score_buf, stat_buf, stat_buf,
                        pltpu.VMEM((2, tk, tq), BF16), stat_buf, pltpu.VMEM((2, v_rows, tq), F32)],
        compiler_params=_params("parallel", "parallel", "arbitrary"),
        name="attn",
    )(q, k1, k2, v, row(lq1), row(lk1), row(lq2), row(lk2), row(subln_g))


def _ssm_weights(a_re, a_im, log_dt, b_re, b_im, c_re, c_im):
    g_n, p_n, h_n = b_re.shape
    gpb = SSM_COL_TILE // h_n
    nb = g_n // gpb
    dt = jnp.exp(log_dt)[:, None]
    mag = jnp.exp(a_re * dt)
    lbar_re = mag * jnp.cos(a_im * dt)
    lbar_im = mag * jnp.sin(a_im * dt)
    nr, ni = lbar_re - 1.0, lbar_im
    den = jnp.square(a_re) + jnp.square(a_im)
    coef_re = ((nr * a_re + ni * a_im) / den)[..., None]
    coef_im = ((ni * a_re - nr * a_im) / den)[..., None]
    bb_re = coef_re * b_re - coef_im * b_im
    bb_im = coef_re * b_im + coef_im * b_re
    eye = jnp.eye(gpb, dtype=F32)

    def in_diag(bb):
        m = bb.reshape(nb, gpb, p_n, h_n).transpose(0, 1, 3, 2)
        return (m[:, :, :, None, :] * eye[None, :, None, :, None]).reshape(nb, gpb * h_n, gpb * p_n)

    def out_diag(c):
        m = c.reshape(nb, gpb, h_n, p_n).transpose(0, 1, 3, 2)
        return (m[:, :, :, None, :] * eye[None, :, None, :, None]).reshape(nb, gpb * p_n, gpb * h_n)

    lam = lambda v: v.reshape(nb, 1, gpb * p_n)
    return (in_diag(bb_re).astype(BF16), in_diag(bb_im).astype(BF16),
            out_diag(c_re).astype(BF16), out_diag(-c_im).astype(BF16), lam(lbar_re), lam(lbar_im))


def _ssm_kernel(s_ref, bre_ref, bim_ref, cre_ref, cim_ref, lre_ref, lim_ref, y_ref,
                slab_scr, sp_scr, xre, xim, loc_re, loc_im, ent_re, ent_im, hre, him, st_re, st_im,
                *, tt, t_len):
    nchunk = tt // t_len
    n_slab = s_ref.shape[1] // LANES
    width = xre.shape[1]

    @pl.when(pl.program_id(2) == 0)
    def _():
        st_re[...] = jnp.zeros(st_re.shape, F32)
        st_im[...] = jnp.zeros(st_im.shape, F32)

    for k in range(n_slab):
        slab_scr[k] = s_ref[:, k * LANES:(k + 1) * LANES]
    for k in range(n_slab):
        for r in range(t_len):
            sp_scr[r * nchunk:(r + 1) * nchunk, k * LANES:(k + 1) * LANES] = (
                slab_scr[k, pl.ds(r, nchunk, stride=t_len), :].astype(BF16))
    sp = sp_scr[...]
    xre[...] = jnp.dot(sp, bre_ref[0], preferred_element_type=F32)
    xim[...] = jnp.dot(sp, bim_ref[0], preferred_element_type=F32)

    def advance(lr, li, hr, hi, rows, lanes):
        return (lr * hr - li * hi + xre[rows, lanes], lr * hi + li * hr + xim[rows, lanes])

    for lb in range(width // LANES):
        lanes = slice(lb * LANES, (lb + 1) * LANES)
        lr, li = lre_ref[0, :, lanes], lim_ref[0, :, lanes]
        hr, hi = xre[0:nchunk, lanes], xim[0:nchunk, lanes]
        for r in range(1, t_len):
            hr, hi = advance(lr, li, hr, hi, slice(r * nchunk, (r + 1) * nchunk), lanes)
        loc_re[:, lanes] = hr
        loc_im[:, lanes] = hi

    dr, di = lre_ref[0], lim_ref[0]
    for _ in range(t_len.bit_length() - 1):
        dr, di = dr * dr - di * di, 2.0 * dr * di

    def chunk_step(c, carry):
        sr, si = carry
        row = pl.ds(c, 1)
        ent_re[row, :] = sr
        ent_im[row, :] = si
        return (dr * sr - di * si + loc_re[row, :], dr * si + di * sr + loc_im[row, :])

    sr, si = lax.fori_loop(0, nchunk, chunk_step, (st_re[...], st_im[...]))
    st_re[...] = sr
    st_im[...] = si

    for lb in range(width // LANES):
        lanes = slice(lb * LANES, (lb + 1) * LANES)
        lr, li = lre_ref[0, :, lanes], lim_ref[0, :, lanes]
        hr, hi = ent_re[:, lanes], ent_im[:, lanes]
        for r in range(t_len):
            rows = slice(r * nchunk, (r + 1) * nchunk)
            hr, hi = advance(lr, li, hr, hi, rows, lanes)
            hre[rows, lanes] = hr.astype(BF16)
            him[rows, lanes] = hi.astype(BF16)

    yp = (jnp.dot(hre[...], cre_ref[0], preferred_element_type=F32)
          + jnp.dot(him[...], cim_ref[0], preferred_element_type=F32))
    for k in range(n_slab):
        for r in range(t_len):
            slab_scr[k, pl.ds(r, nchunk, stride=t_len), :] = yp[r * nchunk:(r + 1) * nchunk,
                                                              k * LANES:(k + 1) * LANES]
        y_ref[:, k * LANES:(k + 1) * LANES] = slab_scr[k]


def _ssm(s2d, wts, *, batch, seq):
    bre, bim, cre, cim, lre, lim = wts
    n, width = s2d.shape
    tt = min(SSM_TOKEN_TILE, seq)
    t_len = SSM_CHUNK
    assert t_len & (t_len - 1) == 0 and tt % t_len == 0 and seq % tt == 0
    tiles = seq // tt
    nchunk = tt // t_len
    col = SSM_COL_TILE
    state_w = bre.shape[2]
    tok_spec = pl.BlockSpec((tt, col), lambda b, j, i: (b * tiles + i, j))
    wspec = lambda shape: pl.BlockSpec((1,) + shape, lambda b, j, i: (j, 0, 0))
    kern = functools.partial(_ssm_kernel, tt=tt, t_len=t_len)
    return pl.pallas_call(
        kern,
        grid=(batch, width // col, tiles),
        in_specs=[tok_spec, wspec((col, state_w)), wspec((col, state_w)),
                  wspec((state_w, col)), wspec((state_w, col)), wspec((1, state_w)), wspec((1, state_w))],
        out_specs=tok_spec,
        out_shape=jax.ShapeDtypeStruct((n, width), F32),
        scratch_shapes=[pltpu.VMEM((col // LANES, tt, LANES), F32), pltpu.VMEM((tt, col), BF16),
                        pltpu.VMEM((tt, state_w), F32), pltpu.VMEM((tt, state_w), F32),
                        pltpu.VMEM((nchunk, state_w), F32), pltpu.VMEM((nchunk, state_w), F32),
                        pltpu.VMEM((nchunk, state_w), F32), pltpu.VMEM((nchunk, state_w), F32),
                        pltpu.VMEM((tt, state_w), BF16), pltpu.VMEM((tt, state_w), BF16),
                        pltpu.VMEM((1, state_w), F32), pltpu.VMEM((1, state_w), F32)],
        compiler_params=_params("parallel", "parallel", "arbitrary"),
        name="ssm",
    )(s2d, bre, bim, cre, cim, lre, lim)


def _gelu_tanh(x):
    return 0.5 * x * (1.0 + jnp.tanh(math.sqrt(2.0 / math.pi) * (x + 0.044715 * (x * x * x))))


def _mixout_kernel(attn_ref, y_ref, s_ref, x_ref, gt_ref, dskip_ref, gluw_ref, glub_ref, wout_ref,
                   g_ref, b_ref, o_ref, *, alpha, width):
    y = y_ref[...] + dskip_ref[...] * s_ref[...]
    ge = _gelu_tanh(y)
    z = jnp.dot(ge.astype(BF16), gluw_ref[...], preferred_element_type=F32) + glub_ref[...]
    ssm = (ge * jax.nn.sigmoid(z)).astype(BF16)
    h = (jnp.dot(attn_ref[...], wout_ref[0:width, :], preferred_element_type=F32)
         + jnp.dot(ssm, wout_ref[width:2 * width, :], preferred_element_type=F32))
    out = alpha * x_ref[...] + (1.0 + gt_ref[0]) * h
    o_ref[...] = _layer_norm_rows(out, g_ref[...], b_ref[...])


def _mixout(attn, y, s2d, x2d, gate, d_skip, glu_w, glu_b, w_out, ln_g, ln_b, *, seq, alpha):
    n, d = x2d.shape
    width = attn.shape[1]
    tm = min(PROJ_TOKEN_TILE, seq)
    tiles_per_seq = seq // tm
    half = pl.BlockSpec((tm, width), lambda i: (i, 0))
    full = pl.BlockSpec((tm, d), lambda i: (i, 0))
    return pl.pallas_call(
        functools.partial(_mixout_kernel, alpha=alpha, width=width),
        grid=(n // tm,),
        in_specs=[half, half, half, full,
                  pl.BlockSpec((1, 1, d), lambda i: (i // tiles_per_seq, 0, 0)),
                  _const_spec((1, width)), _const_spec(glu_w.shape), _const_spec((1, width)),
                  _const_spec(w_out.shape), _const_spec((1, d)), _const_spec((1, d))],
        out_specs=full,
        out_shape=jax.ShapeDtypeStruct((n, d), F32),
        compiler_params=_params("parallel"),
        name="mixout",
    )(attn, y, s2d, x2d, gate, d_skip.reshape(1, width), glu_w, glu_b.reshape(1, width), w_out,
      ln_g.reshape(1, d), ln_b.reshape(1, d))


def kernel(x, c, w_cond, b_cond, ffn1_w1, ffn1_w3, ffn1_w2, w_in, lambda_q1, lambda_k1, lambda_q2, lambda_k2, subln_g, ssm_a_re, ssm_a_im, ssm_log_dt, ssm_b_re, ssm_b_im, ssm_c_re, ssm_c_im, ssm_d, glu_w, glu_b, w_out, ffn2_w1, ffn2_w3, ffn2_w2, ln_g, ln_b):
    batch, seq, d = x.shape
    depth = w_cond.shape[0]
    alpha = (2 * depth) ** 0.25
    h = x.reshape(batch * seq, d)
    for i in range(depth):
        lambda_init = 0.8 - 0.6 * math.exp(-0.3 * i)
        mod = _cond(c, w_cond[i], b_cond[i]).reshape(batch, N_SUBLAYERS, 3, 1, d)
        shift, scale, gate = (lambda k: mod[:, k, 0]), (lambda k: mod[:, k, 1]), (lambda k: mod[:, k, 2])

        h = _ffn(h, shift(0), scale(0), gate(0), ffn1_w1[i].astype(BF16), ffn1_w3[i].astype(BF16),
                 ffn1_w2[i].astype(BF16), ln_g[i, 0], ln_b[i, 0], seq=seq, alpha=alpha)

        q, k1, k2, v, s = _inproj(h, shift(1), scale(1), w_in[i].astype(BF16), seq=seq)
        attn = _attention(q, k1, k2, v, lambda_q1[i], lambda_k1[i], lambda_q2[i], lambda_k2[i],
                          subln_g[i], batch=batch, seq=seq, lambda_init=lambda_init)
        wts = _ssm_weights(ssm_a_re[i], ssm_a_im[i], ssm_log_dt[i], ssm_b_re[i], ssm_b_im[i],
                           ssm_c_re[i], ssm_c_im[i])
        y = _ssm(s, wts, batch=batch, seq=seq)
        h = _mixout(attn, y, s, h, gate(1), ssm_d[i], glu_w[i].astype(BF16), glu_b[i],
                    w_out[i].astype(BF16), ln_g[i, 1], ln_b[i, 1], seq=seq, alpha=alpha)

        h = _ffn(h, shift(2), scale(2), gate(2), ffn2_w1[i].astype(BF16), ffn2_w3[i].astype(BF16),
                 ffn2_w2[i].astype(BF16), ln_g[i, 2], ln_b[i, 2], seq=seq, alpha=alpha)
    return h.reshape(batch, seq, d)
```

```python
import functools
import math

import jax
import jax.numpy as jnp
from jax import lax
from jax.experimental import pallas as pl
from jax.experimental.pallas import tpu as pltpu

F32 = jnp.float32
BF16 = jnp.bfloat16

N_SUBLAYERS = 3
DIFF_HEAD_DIM = 64
SSM_GROUP = 16
SSM_STATE = 64
LN_EPS = 1e-5
RMS_EPS = 1e-5
FFN_RES_WEIGHT = 0.5
SSM_CHUNK = 16
SSM_TOKEN_TILE = 1024
SSM_COL_TILE = 256

V7X_VMEM_BYTES = 64 * 1024 * 1024
VMEM_LIMIT = V7X_VMEM_BYTES - 4 * 1024 * 1024
LANES = 128

FFN_TOKEN_TILE = 1024
FFN_HIDDEN_TILE = 512
FFN_OUT_CHUNK = 512
ROW_CHUNK = 256
PROJ_TOKEN_TILE = 512
ATTN_KV_TILE = 512
ATTN_ROW_STRIP = 64
LOG2E = 1.4426950408889634
COND_COL_TILE = 1024
NEG_BIG = -1e30


def _params(*sem):
    return pltpu.CompilerParams(dimension_semantics=sem, vmem_limit_bytes=VMEM_LIMIT)


def _const_spec(shape):
    nd = len(shape)
    return pl.BlockSpec(shape, lambda *_: (0,) * nd, pipeline_mode=pl.Buffered(1))


def _layer_norm_rows(y, g, b):
    mu = jnp.mean(y, axis=-1, keepdims=True)
    yc = y - mu
    var = jnp.mean(yc * yc, axis=-1, keepdims=True)
    return yc * lax.rsqrt(var + LN_EPS) * g + b


def _cond_kernel(c_ref, w_ref, b_ref, o_ref):
    c = c_ref[...]
    a = c * jax.nn.sigmoid(c)
    o_ref[...] = jnp.dot(a, w_ref[...], preferred_element_type=F32,
                         precision=lax.Precision.HIGHEST) + b_ref[...]


def _cond(c, w_cond, b_cond):
    bsz, d = c.shape
    ncol = w_cond.shape[1]
    rows = 8
    c_pad = jnp.zeros((rows, d), F32).at[:bsz].set(c)
    tn = COND_COL_TILE
    out = pl.pallas_call(
        _cond_kernel,
        grid=(ncol // tn,),
        in_specs=[pl.BlockSpec((rows, d), lambda j: (0, 0)),
                  pl.BlockSpec((d, tn), lambda j: (0, j)),
                  pl.BlockSpec((1, tn), lambda j: (0, j))],
        out_specs=pl.BlockSpec((rows, tn), lambda j: (0, j)),
        out_shape=jax.ShapeDtypeStruct((rows, ncol), F32),
        compiler_params=_params("arbitrary"),
        name="cond",
    )(c_pad, w_cond, b_cond.reshape(1, ncol))
    return out[:bsz]


def _ffn_kernel(x_ref, sh_ref, sc_ref, gt_ref, w1_ref, w3_ref, w2_ref, g_ref, b_ref,
                o_ref, u_ref, *, alpha, n_hidden_tiles, tm, d):
    j = pl.program_id(1)

    @pl.when(j == 0)
    def _():
        for r in range(0, tm, ROW_CHUNK):
            rows = pl.ds(r, ROW_CHUNK)
            u_ref[rows, :] = (x_ref[rows, :] * (1.0 + sc_ref[0]) + sh_ref[0]).astype(BF16)

    u = u_ref[...]
    h1 = jnp.dot(u, w1_ref[...], preferred_element_type=F32)
    h3 = jnp.dot(u, w3_ref[...], preferred_element_type=F32)
    a = (h1 * jax.nn.sigmoid(h1) * h3).astype(BF16)

    @pl.when(j == 0)
    def _():
        for n in range(0, d, FFN_OUT_CHUNK):
            cols = pl.ds(n, FFN_OUT_CHUNK)
            o_ref[:, cols] = jnp.dot(a, w2_ref[:, cols], preferred_element_type=F32)

    @pl.when(j > 0)
    def _():
        for n in range(0, d, FFN_OUT_CHUNK):
            cols = pl.ds(n, FFN_OUT_CHUNK)
            o_ref[:, cols] += jnp.dot(a, w2_ref[:, cols], preferred_element_type=F32)

    @pl.when(j == n_hidden_tiles - 1)
    def _():
        gate = FFN_RES_WEIGHT * (1.0 + gt_ref[0])
        for r in range(0, tm, ROW_CHUNK):
            rows = pl.ds(r, ROW_CHUNK)
            y = alpha * x_ref[rows, :] + gate * o_ref[rows, :]
            o_ref[rows, :] = _layer_norm_rows(y, g_ref[...], b_ref[...])


def _ffn(x2d, shift, scale, gate, w1, w3, w2, ln_g, ln_b, *, seq, alpha):
    n, d = x2d.shape
    f = w1.shape[1]
    tm = min(FFN_TOKEN_TILE, seq)
    tf = FFN_HIDDEN_TILE
    tiles_per_seq = seq // tm
    nj = f // tf
    mod_spec = pl.BlockSpec((1, 1, d), lambda i, j: (i // tiles_per_seq, 0, 0))
    vec_spec = pl.BlockSpec((1, d), lambda i, j: (0, 0))
    kern = functools.partial(_ffn_kernel, alpha=alpha, n_hidden_tiles=nj, tm=tm, d=d)
    return pl.pallas_call(
        kern,
        grid=(n // tm, nj),
        in_specs=[pl.BlockSpec((tm, d), lambda i, j: (i, 0)),
                  mod_spec, mod_spec, mod_spec,
                  pl.BlockSpec((d, tf), lambda i, j: (0, j)),
                  pl.BlockSpec((d, tf), lambda i, j: (0, j)),
                  pl.BlockSpec((tf, d), lambda i, j: (j, 0)),
                  vec_spec, vec_spec],
        out_specs=pl.BlockSpec((tm, d), lambda i, j: (i, 0)),
        out_shape=jax.ShapeDtypeStruct((n, d), F32),
        scratch_shapes=[pltpu.VMEM((tm, d), BF16)],
        compiler_params=_params("parallel", "arbitrary"),
        name="ffn",
    )(x2d, shift, scale, gate, w1, w3, w2, ln_g.reshape(1, d), ln_b.reshape(1, d))


def _inproj_kernel(x_ref, sh_ref, sc_ref, w_ref, q_ref, k1_ref, k2_ref, v_ref, s_ref, *, width):
    u = (x_ref[...] * (1.0 + sc_ref[0]) + sh_ref[0]).astype(BF16)
    qk_scale = LOG2E / math.sqrt(DIFF_HEAD_DIM)
    q = jnp.dot(u, w_ref[:, 0:width], preferred_element_type=F32)
    q_ref[...] = (q * qk_scale).astype(BF16)
    k = jnp.dot(u, w_ref[:, width:2 * width], preferred_element_type=F32)
    lane = lax.broadcasted_iota(jnp.int32, k.shape, 1)
    first = (lane & DIFF_HEAD_DIM) == 0
    k1_ref[...] = jnp.where(first, k, 0.0).astype(BF16)
    k2_ref[...] = jnp.where(first, 0.0, k).astype(BF16)
    v_ref[...] = jnp.dot(u, w_ref[:, 2 * width:3 * width], preferred_element_type=F32).astype(BF16)
    s_ref[...] = jnp.dot(u, w_ref[:, 3 * width:4 * width], preferred_element_type=F32)


def _inproj(x2d, shift, scale, w_in, *, seq):
    n, d = x2d.shape
    width = w_in.shape[1] // 4
    tm = min(PROJ_TOKEN_TILE, seq)
    tiles_per_seq = seq // tm
    mod_spec = pl.BlockSpec((1, 1, d), lambda i: (i // tiles_per_seq, 0, 0))
    out_spec = pl.BlockSpec((tm, width), lambda i: (i, 0))
    bf = jax.ShapeDtypeStruct((n, width), BF16)
    return pl.pallas_call(
        functools.partial(_inproj_kernel, width=width),
        grid=(n // tm,),
        in_specs=[pl.BlockSpec((tm, d), lambda i: (i, 0)), mod_spec, mod_spec,
                  _const_spec(w_in.shape)],
        out_specs=[out_spec] * 5,
        out_shape=[bf, bf, bf, bf, jax.ShapeDtypeStruct((n, width), F32)],
        compiler_params=_params("parallel"),
        name="inproj",
    )(x2d, shift, scale, w_in)


def _attn_kernel(q_ref, k1_ref, k2_ref, v_ref, lq1_ref, lk1_ref, lq2_ref, lk2_ref, g_ref,
                 o_ref, vt_scr, sa_scr, sb_scr, pma_scr, pmb_scr, p_scr, m_scr, acc_scr,
                 *, tq, tk, lambda_init):
    qi = pl.program_id(2)
    nt = (((1,), (1,)), ((), ()))
    k_refs = (k1_ref, k2_ref)
    n_kv = vt_scr.shape[0]
    v_rows = vt_scr.shape[1]

    @pl.when(qi == 0)
    def _():
        for kb in range(n_kv):
            vt = jnp.transpose(v_ref[kb * tk:(kb + 1) * tk, :].astype(F32))
            vt_scr[kb, 0:LANES, :] = vt.astype(BF16)
            vt_scr[kb, LANES:v_rows, :] = jnp.ones((v_rows - LANES, tk), BF16)

    m_scr[...] = jnp.full(m_scr.shape, NEG_BIG, F32)
    acc_scr[...] = jnp.zeros(acc_scr.shape, F32)

    def row_max(s):
        return jnp.max(s.reshape(tk // 8, 8, tq), axis=0)

    def scores(kb, s_scr, pm_scr):
        rows_k = pl.ds(pl.multiple_of(kb * tk, tk), tk)
        q = q_ref[...]
        for mp in range(2):
            s = lax.dot_general(k_refs[mp][rows_k, :], q, nt, preferred_element_type=F32)
            s_scr[mp] = s
            pm_scr[mp] = row_max(s)

    def consume(kb, s_scr, pm_scr, masked):
        vt = vt_scr[kb]
        for mp in range(2):
            if masked:
                r = lax.broadcasted_iota(jnp.int32, (tk, tq), 0)
                c = lax.broadcasted_iota(jnp.int32, (tk, tq), 1)
                s = jnp.where(r - c <= qi * tq - kb * tk, s_scr[mp], -jnp.inf)
                s_scr[mp] = s
                pm = row_max(s)
            else:
                pm = pm_scr[mp]
            m_old = m_scr[mp]
            m_new = jnp.maximum(m_old, jnp.broadcast_to(jnp.max(pm, axis=0, keepdims=True), (8, tq)))
            m_scr[mp] = m_new
            corr = jnp.exp2(m_old - m_new)
            m_rows = jnp.concatenate([m_new] * (ATTN_ROW_STRIP // 8), axis=0)
            for r0 in range(0, tk, ATTN_ROW_STRIP):
                rs = pl.ds(r0, ATTN_ROW_STRIP)
                p_scr[mp, rs, :] = jnp.exp2(s_scr[mp, rs, :] - m_rows).astype(BF16)
            res = jnp.dot(vt, p_scr[mp], preferred_element_type=F32)
            acc_scr[mp] = jnp.concatenate([corr] * (v_rows // 8), axis=0) * acc_scr[mp] + res

    scores(0, sa_scr, pma_scr)

    def pair(kb):
        scores(kb + 1, sb_scr, pmb_scr)
        consume(kb, sa_scr, pma_scr, False)
        scores(kb + 2, sa_scr, pma_scr)
        consume(kb + 1, sb_scr, pmb_scr, False)

    def body(j, c):
        pair(4 * j)
        pair(4 * j + 2)
        return c
    lax.fori_loop(0, qi // 2, body, 0)

    @pl.when(qi % 2 == 1)
    def _():
        pair(2 * qi - 2)

    consume(2 * qi, sa_scr, pma_scr, True)

    kb = 2 * qi + 1
    rows_k = pl.ds(pl.multiple_of(kb * tk, tk), tk)
    q_late = q_ref[tk:, :]
    vt = vt_scr[kb]
    r = lax.broadcasted_iota(jnp.int32, (tk, tk), 0)
    c = lax.broadcasted_iota(jnp.int32, (tk, tk), 1)
    for mp in range(2):
        s = lax.dot_general(k_refs[mp][rows_k, :], q_late, nt, preferred_element_type=F32)
        s = jnp.where(r <= c, s, -jnp.inf)
        sb_scr[mp, :, 0:tk] = s
        m_old = m_scr[mp, :, tk:]
        col_max = jnp.max(jnp.max(s.reshape(tk // 8, 8, tk), axis=0), axis=0, keepdims=True)
        m_new = jnp.maximum(m_old, jnp.broadcast_to(col_max, (8, tk)))
        m_scr[mp, :, tk:] = m_new
        corr = jnp.exp2(m_old - m_new)
        m_rows = jnp.concatenate([m_new] * (ATTN_ROW_STRIP // 8), axis=0)
        for r0 in range(0, tk, ATTN_ROW_STRIP):
            rs = pl.ds(r0, ATTN_ROW_STRIP)
            p_scr[mp, rs, 0:tk] = jnp.exp2(sb_scr[mp, rs, 0:tk] - m_rows).astype(BF16)
        res = jnp.dot(vt, p_scr[mp, :, 0:tk], preferred_element_type=F32)
        acc_scr[mp, :, tk:] = (jnp.concatenate([corr] * (v_rows // 8), axis=0) * acc_scr[mp, :, tk:] + res)

    lam = (jnp.exp(jnp.sum(lq1_ref[...] * lk1_ref[...], axis=-1, keepdims=True))
           - jnp.exp(jnp.sum(lq2_ref[...] * lk2_ref[...], axis=-1, keepdims=True)) + lambda_init)

    def normalised(mp):
        denom = jnp.concatenate([acc_scr[mp, LANES:LANES + 8, :]] * (LANES // 8), axis=0)
        return acc_scr[mp, 0:LANES, :] / denom

    o = jnp.transpose(normalised(0) - lam * normalised(1))
    o = o * lax.rsqrt(jnp.mean(o * o, axis=-1, keepdims=True) + RMS_EPS)
    o_ref[...] = (o * g_ref[...] * (1.0 - lambda_init)).astype(BF16)


def _attention(q, k1, k2, v, lq1, lk1, lq2, lk2, subln_g, *, batch, seq, lambda_init):
    n, width = q.shape
    heads = width // LANES
    tk = min(ATTN_KV_TILE, seq // 2)
    tq = 2 * tk
    nq = seq // tq
    v_rows = LANES + 16
    kv_spec = pl.BlockSpec((seq, LANES), lambda b, h, i: (b, h))
    lam_spec = pl.BlockSpec((1, DIFF_HEAD_DIM), lambda b, h, i: (0, 0))
    q_spec = pl.BlockSpec((tq, LANES), lambda b, h, i: (b * nq + i, h))
    kern = functools.partial(_attn_kernel, tq=tq, tk=tk, lambda_init=lambda_init)
    row = lambda a: a.reshape(1, -1).astype(F32)
    score_buf = pltpu.VMEM((2, tk, tq), F32)
    stat_buf = pltpu.VMEM((2, 8, tq), F32)
    return pl.pallas_call(
        kern,
        grid=(batch, heads, nq),
        in_specs=[q_spec, kv_spec, kv_spec, kv_spec, lam_spec, lam_spec, lam_spec, lam_spec,
                  pl.BlockSpec((1, LANES), lambda b, h, i: (0, 0))],
        out_specs=q_spec,
        out_shape=jax.ShapeDtypeStruct((n, width), BF16),
        scratch_shapes=[pltpu.VMEM((seq // tk, v_rows, tk), BF16), score_buf, score_buf, stat_buf, stat_buf,
                        pltpu.VMEM((2, tk, tq), BF16), stat_buf, pltpu.VMEM((2, v_rows, tq), F32)],
        compiler_params=_params("parallel", "parallel", "arbitrary"),
        name="attn",
    )(q, k1, k2, v, row(lq1), row(lk1), row(lq2), row(lk2), row(subln_g))


def _ssm_weights(a_re, a_im, log_dt, b_re, b_im, c_re, c_im):
    g_n, p_n, h_n = b_re.shape
    gpb = SSM_COL_TILE // h_n
    nb = g_n // gpb
    dt = jnp.exp(log_dt)[:, None]
    mag = jnp.exp(a_re * dt)
    lbar_re = mag * jnp.cos(a_im * dt)
    lbar_im = mag * jnp.sin(a_im * dt)
    nr, ni = lbar_re - 1.0, lbar_im
    den = jnp.square(a_re) + jnp.square(a_im)
    coef_re = ((nr * a_re + ni * a_im) / den)[..., None]
    coef_im = ((ni * a_re - nr * a_im) / den)[..., None]
    bb_re = coef_re * b_re - coef_im * b_im
    bb_im = coef_re * b_im + coef_im * b_re
    eye = jnp.eye(gpb, dtype=F32)

    def in_diag(bb):
        m = bb.reshape(nb, gpb, p_n, h_n).transpose(0, 1, 3, 2)
        return (m[:, :, :, None, :] * eye[None, :, None, :, None]).reshape(nb, gpb * h_n, gpb * p_n)

    def out_diag(c):
        m = c.reshape(nb, gpb, h_n, p_n).transpose(0, 1, 3, 2)
        return (m[:, :, :, None, :] * eye[None, :, None, :, None]).reshape(nb, gpb * p_n, gpb * h_n)

    lam = lambda v: v.reshape(nb, 1, gpb * p_n)
    return (in_diag(bb_re).astype(BF16), in_diag(bb_im).astype(BF16),
            out_diag(c_re).astype(BF16), out_diag(-c_im).astype(BF16), lam(lbar_re), lam(lbar_im))


def _ssm_kernel(s_ref, bre_ref, bim_ref, cre_ref, cim_ref, lre_ref, lim_ref, y_ref,
                slab_scr, sp_scr, xre, xim, loc_re, loc_im, ent_re, ent_im, hre, him, st_re, st_im,
                *, tt, t_len):
    nchunk = tt // t_len
    n_slab = s_ref.shape[1] // LANES
    width = xre.shape[1]

    @pl.when(pl.program_id(2) == 0)
    def _():
        st_re[...] = jnp.zeros(st_re.shape, F32)
        st_im[...] = jnp.zeros(st_im.shape, F32)

    for k in range(n_slab):
        slab_scr[k] = s_ref[:, k * LANES:(k + 1) * LANES]
    for k in range(n_slab):
        for r in range(t_len):
            sp_scr[r * nchunk:(r + 1) * nchunk, k * LANES:(k + 1) * LANES] = (
                slab_scr[k, pl.ds(r, nchunk, stride=t_len), :].astype(BF16))
    sp = sp_scr[...]
    xre[...] = jnp.dot(sp, bre_ref[0], preferred_element_type=F32)
    xim[...] = jnp.dot(sp, bim_ref[0], preferred_element_type=F32)

    def advance(lr, li, hr, hi, rows, lanes):
        return (lr * hr - li * hi + xre[rows, lanes], lr * hi + li * hr + xim[rows, lanes])

    for lb in range(width // LANES):
        lanes = slice(lb * LANES, (lb + 1) * LANES)
        lr, li = lre_ref[0, :, lanes], lim_ref[0, :, lanes]
        hr, hi = xre[0:nchunk, lanes], xim[0:nchunk, lanes]
        for r in range(1, t_len):
            hr, hi = advance(lr, li, hr, hi, slice(r * nchunk, (r + 1) * nchunk), lanes)
        loc_re[:, lanes] = hr
        loc_im[:, lanes] = hi

    dr, di = lre_ref[0], lim_ref[0]
    for _ in range(t_len.bit_length() - 1):
        dr, di = dr * dr - di * di, 2.0 * dr * di

    def chunk_step(c, carry):
        sr, si = carry
        row = pl.ds(c, 1)
        ent_re[row, :] = sr
        ent_im[row, :] = si
        return (dr * sr - di * si + loc_re[row, :], dr * si + di * sr + loc_im[row, :])

    sr, si = lax.fori_loop(0, nchunk, chunk_step, (st_re[...], st_im[...]))
    st_re[...] = sr
    st_im[...] = si

    for lb in range(width // LANES):
        lanes = slice(lb * LANES, (lb + 1) * LANES)
        lr, li = lre_ref[0, :, lanes], lim_ref[0, :, lanes]
        hr, hi = ent_re[:, lanes], ent_im[:, lanes]
        for r in range(t_len):
            rows = slice(r * nchunk, (r + 1) * nchunk)
            hr, hi = advance(lr, li, hr, hi, rows, lanes)
            hre[rows, lanes] = hr.astype(BF16)
            him[rows, lanes] = hi.astype(BF16)

    yp = (jnp.dot(hre[...], cre_ref[0], preferred_element_type=F32)
          + jnp.dot(him[...], cim_ref[0], preferred_element_type=F32))
    for k in range(n_slab):
        for r in range(t_len):
            slab_scr[k, pl.ds(r, nchunk, stride=t_len), :] = yp[r * nchunk:(r + 1) * nchunk,
                                                              k * LANES:(k + 1) * LANES]
        y_ref[:, k * LANES:(k + 1) * LANES] = slab_scr[k]


def _ssm(s2d, wts, *, batch, seq):
    bre, bim, cre, cim, lre, lim = wts
    n, width = s2d.shape
    tt = min(SSM_TOKEN_TILE, seq)
    t_len = SSM_CHUNK
    assert t_len & (t_len - 1) == 0 and tt % t_len == 0 and seq % tt == 0
    tiles = seq // tt
    nchunk = tt // t_len
    col = SSM_COL_TILE
    state_w = bre.shape[2]
    tok_spec = pl.BlockSpec((tt, col), lambda b, j, i: (b * tiles + i, j))
    wspec = lambda shape: pl.BlockSpec((1,) + shape, lambda b, j, i: (j, 0, 0))
    kern = functools.partial(_ssm_kernel, tt=tt, t_len=t_len)
    return pl.pallas_call(
        kern,
        grid=(batch, width // col, tiles),
        in_specs=[tok_spec, wspec((col, state_w)), wspec((col, state_w)),
                  wspec((state_w, col)), wspec((state_w, col)), wspec((1, state_w)), wspec((1, state_w))],
        out_specs=tok_spec,
        out_shape=jax.ShapeDtypeStruct((n, width), F32),
        scratch_shapes=[pltpu.VMEM((col // LANES, tt, LANES), F32), pltpu.VMEM((tt, col), BF16),
                        pltpu.VMEM((tt, state_w), F32), pltpu.VMEM((tt, state_w), F32),
                        pltpu.VMEM((nchunk, state_w), F32), pltpu.VMEM((nchunk, state_w), F32),
                        pltpu.VMEM((nchunk, state_w), F32), pltpu.VMEM((nchunk, state_w), F32),
                        pltpu.VMEM((tt, state_w), BF16), pltpu.VMEM((tt, state_w), BF16),
                        pltpu.VMEM((1, state_w), F32), pltpu.VMEM((1, state_w), F32)],
        compiler_params=_params("parallel", "parallel", "arbitrary"),
        name="ssm",
    )(s2d, bre, bim, cre, cim, lre, lim)


def _gelu_tanh(x):
    return 0.5 * x * (1.0 + jnp.tanh(math.sqrt(2.0 / math.pi) * (x + 0.044715 * (x * x * x))))


def _mixout_kernel(attn_ref, y_ref, s_ref, x_ref, gt_ref, dskip_ref, gluw_ref, glub_ref, wout_ref,
                   g_ref, b_ref, o_ref, *, alpha, width):
    y = y_ref[...] + dskip_ref[...] * s_ref[...]
    ge = _gelu_tanh(y)
    z = jnp.dot(ge.astype(BF16), gluw_ref[...], preferred_element_type=F32) + glub_ref[...]
    ssm = (ge * jax.nn.sigmoid(z)).astype(BF16)
    h = (jnp.dot(attn_ref[...], wout_ref[0:width, :], preferred_element_type=F32)
         + jnp.dot(ssm, wout_ref[width:2 * width, :], preferred_element_type=F32))
    out = alpha * x_ref[...] + (1.0 + gt_ref[0]) * h
    o_ref[...] = _layer_norm_rows(out, g_ref[...], b_ref[...])


def _mixout(attn, y, s2d, x2d, gate, d_skip, glu_w, glu_b, w_out, ln_g, ln_b, *, seq, alpha):
    n, d = x2d.shape
    width = attn.shape[1]
    tm = min(PROJ_TOKEN_TILE, seq)
    tiles_per_seq = seq // tm
    half = pl.BlockSpec((tm, width), lambda i: (i, 0))
    full = pl.BlockSpec((tm, d), lambda i: (i, 0))
    return pl.pallas_call(
        functools.partial(_mixout_kernel, alpha=alpha, width=width),
        grid=(n // tm,),
        in_specs=[half, half, half, full,
                  pl.BlockSpec((1, 1, d), lambda i: (i // tiles_per_seq, 0, 0)),
                  _const_spec((1, width)), _const_spec(glu_w.shape), _const_spec((1, width)),
                  _const_spec(w_out.shape), _const_spec((1, d)), _const_spec((1, d))],
        out_specs=full,
        out_shape=jax.ShapeDtypeStruct((n, d), F32),
        compiler_params=_params("parallel"),
        name="mixout",
    )(attn, y, s2d, x2d, gate, d_skip.reshape(1, width), glu_w, glu_b.reshape(1, width), w_out,
      ln_g.reshape(1, d), ln_b.reshape(1, d))


def kernel(x, c, w_cond, b_cond, ffn1_w1, ffn1_w3, ffn1_w2, w_in, lambda_q1, lambda_k1, lambda_q2, lambda_k2, subln_g, ssm_a_re, ssm_a_im, ssm_log_dt, ssm_b_re, ssm_b_im, ssm_c_re, ssm_c_im, ssm_d, glu_w, glu_b, w_out, ffn2_w1, ffn2_w3, ffn2_w2, ln_g, ln_b):
    batch, seq, d = x.shape
    depth = w_cond.shape[0]
    alpha = (2 * depth) ** 0.25
    h = x.reshape(batch * seq, d)
    for i in range(depth):
        lambda_init = 0.8 - 0.6 * math.exp(-0.3 * i)
        mod = _cond(c, w_cond[i], b_cond[i]).reshape(batch, N_SUBLAYERS, 3, 1, d)
        shift, scale, gate = (lambda k: mod[:, k, 0]), (lambda k: mod[:, k, 1]), (lambda k: mod[:, k, 2])

        h = _ffn(h, shift(0), scale(0), gate(0), ffn1_w1[i].astype(BF16), ffn1_w3[i].astype(BF16),
                 ffn1_w2[i].astype(BF16), ln_g[i, 0], ln_b[i, 0], seq=seq, alpha=alpha)

        q, k1, k2, v, s = _inproj(h, shift(1), scale(1), w_in[i].astype(BF16), seq=seq)
        attn = _attention(q, k1, k2, v, lambda_q1[i], lambda_k1[i], lambda_q2[i], lambda_k2[i],
                          subln_g[i], batch=batch, seq=seq, lambda_init=lambda_init)
        wts = _ssm_weights(ssm_a_re[i], ssm_a_im[i], ssm_log_dt[i], ssm_b_re[i], ssm_b_im[i],
                           ssm_c_re[i], ssm_c_im[i])
        y = _ssm(s, wts, batch=batch, seq=seq)
        h = _mixout(attn, y, s, h, gate(1), ssm_d[i], glu_w[i].astype(BF16), glu_b[i],
                    w_out[i].astype(BF16), ln_g[i, 1], ln_b[i, 1], seq=seq, alpha=alpha)

        h = _ffn(h, shift(2), scale(2), gate(2), ffn2_w1[i].astype(BF16), ffn2_w3[i].astype(BF16),
                 ffn2_w2[i].astype(BF16), ln_g[i, 2], ln_b[i, 2], seq=seq, alpha=alpha)
    return h.reshape(batch, seq, d)
```

```python
import functools
import math

import jax
import jax.numpy as jnp
from jax import lax
from jax.experimental import pallas as pl
from jax.experimental.pallas import tpu as pltpu

F32 = jnp.float32
BF16 = jnp.bfloat16

N_SUBLAYERS = 3
DIFF_HEAD_DIM = 64
SSM_GROUP = 16
SSM_STATE = 64
LN_EPS = 1e-5
RMS_EPS = 1e-5
FFN_RES_WEIGHT = 0.5
SSM_CHUNK = 16
SSM_TOKEN_TILE = 1024
SSM_COL_TILE = 256

V7X_VMEM_BYTES = 64 * 1024 * 1024
VMEM_LIMIT = V7X_VMEM_BYTES - 4 * 1024 * 1024
LANES = 128

FFN_TOKEN_TILE = 1024
FFN_HIDDEN_TILE = 512
FFN_OUT_CHUNK = 512
ROW_CHUNK = 256
PROJ_TOKEN_TILE = 512
ATTN_KV_TILE = 512
ATTN_ROW_STRIP = 64
LOG2E = 1.4426950408889634
COND_COL_TILE = 1024
NEG_BIG = -1e30


def _params(*sem):
    return pltpu.CompilerParams(dimension_semantics=sem, vmem_limit_bytes=VMEM_LIMIT)


def _const_spec(shape):
    nd = len(shape)
    return pl.BlockSpec(shape, lambda *_: (0,) * nd, pipeline_mode=pl.Buffered(1))


def _cast_specs(weights, n_steps, step_index):
    specs, shapes = [], []
    for w in weights:
        rows, cols = w.shape
        assert rows % (16 * n_steps) == 0, (w.shape, n_steps)
        specs.append(pl.BlockSpec((rows // n_steps, cols), lambda *idx: (step_index(*idx), 0)))
        shapes.append(jax.ShapeDtypeStruct(w.shape, BF16))
    return specs, shapes


def _cast_blocks(srcs, dsts):
    for src, dst in zip(srcs, dsts):
        dst[...] = src[...].astype(BF16)


def _layer_norm_rows(y, g, b):
    mu = jnp.mean(y, axis=-1, keepdims=True)
    yc = y - mu
    var = jnp.mean(yc * yc, axis=-1, keepdims=True)
    return yc * lax.rsqrt(var + LN_EPS) * g + b


def _cond_kernel(c_ref, w_ref, b_ref, o_ref):
    c = c_ref[...]
    a = c * jax.nn.sigmoid(c)
    o_ref[...] = jnp.dot(a, w_ref[...], preferred_element_type=F32,
                         precision=lax.Precision.HIGHEST) + b_ref[...]


def _cond(c, w_cond, b_cond):
    bsz, d = c.shape
    ncol = w_cond.shape[1]
    rows = 8
    c_pad = jnp.zeros((rows, d), F32).at[:bsz].set(c)
    tn = COND_COL_TILE
    out = pl.pallas_call(
        _cond_kernel,
        grid=(ncol // tn,),
        in_specs=[pl.BlockSpec((rows, d), lambda j: (0, 0)),
                  pl.BlockSpec((d, tn), lambda j: (0, j)),
                  pl.BlockSpec((1, tn), lambda j: (0, j))],
        out_specs=pl.BlockSpec((rows, tn), lambda j: (0, j)),
        out_shape=jax.ShapeDtypeStruct((rows, ncol), F32),
        compiler_params=_params("arbitrary"),
        name="cond",
    )(c_pad, w_cond, b_cond.reshape(1, ncol))
    return out[:bsz]


def _ffn_kernel(x_ref, sh_ref, sc_ref, gt_ref, w1_ref, w3_ref, w2_ref, g_ref, b_ref,
                o_ref, u_ref, *, alpha, n_hidden_tiles, tm, d):
    j = pl.program_id(1)

    @pl.when(j == 0)
    def _():
        for r in range(0, tm, ROW_CHUNK):
            rows = pl.ds(r, ROW_CHUNK)
            u_ref[rows, :] = (x_ref[rows, :] * (1.0 + sc_ref[0]) + sh_ref[0]).astype(BF16)

    u = u_ref[...]
    h1 = jnp.dot(u, w1_ref[...], preferred_element_type=F32)
    h3 = jnp.dot(u, w3_ref[...], preferred_element_type=F32)
    a = (h1 * jax.nn.sigmoid(h1) * h3).astype(BF16)

    @pl.when(j == 0)
    def _():
        for n in range(0, d, FFN_OUT_CHUNK):
            cols = pl.ds(n, FFN_OUT_CHUNK)
            o_ref[:, cols] = jnp.dot(a, w2_ref[:, cols], preferred_element_type=F32)

    @pl.when(j > 0)
    def _():
        for n in range(0, d, FFN_OUT_CHUNK):
            cols = pl.ds(n, FFN_OUT_CHUNK)
            o_ref[:, cols] += jnp.dot(a, w2_ref[:, cols], preferred_element_type=F32)

    @pl.when(j == n_hidden_tiles - 1)
    def _():
        gate = FFN_RES_WEIGHT * (1.0 + gt_ref[0])
        for r in range(0, tm, ROW_CHUNK):
            rows = pl.ds(r, ROW_CHUNK)
            y = alpha * x_ref[rows, :] + gate * o_ref[rows, :]
            o_ref[rows, :] = _layer_norm_rows(y, g_ref[...], b_ref[...])


def _ffn(x2d, shift, scale, gate, w1, w3, w2, ln_g, ln_b, *, seq, alpha):
    n, d = x2d.shape
    f = w1.shape[1]
    tm = min(FFN_TOKEN_TILE, seq)
    tf = FFN_HIDDEN_TILE
    tiles_per_seq = seq // tm
    nj = f // tf
    mod_spec = pl.BlockSpec((1, 1, d), lambda i, j: (i // tiles_per_seq, 0, 0))
    vec_spec = pl.BlockSpec((1, d), lambda i, j: (0, 0))
    kern = functools.partial(_ffn_kernel, alpha=alpha, n_hidden_tiles=nj, tm=tm, d=d)
    return pl.pallas_call(
        kern,
        grid=(n // tm, nj),
        in_specs=[pl.BlockSpec((tm, d), lambda i, j: (i, 0)),
                  mod_spec, mod_spec, mod_spec,
                  pl.BlockSpec((d, tf), lambda i, j: (0, j)),
                  pl.BlockSpec((d, tf), lambda i, j: (0, j)),
                  pl.BlockSpec((tf, d), lambda i, j: (j, 0)),
                  vec_spec, vec_spec],
        out_specs=pl.BlockSpec((tm, d), lambda i, j: (i, 0)),
        out_shape=jax.ShapeDtypeStruct((n, d), F32),
        scratch_shapes=[pltpu.VMEM((tm, d), BF16)],
        compiler_params=_params("parallel", "arbitrary"),
        name="ffn",
    )(x2d, shift, scale, gate, w1, w3, w2, ln_g.reshape(1, d), ln_b.reshape(1, d))


def _inproj_kernel(x_ref, sh_ref, sc_ref, w_ref, q_ref, k1_ref, k2_ref, v_ref, s_ref, *, width):
    u = (x_ref[...] * (1.0 + sc_ref[0]) + sh_ref[0]).astype(BF16)
    qk_scale = LOG2E / math.sqrt(DIFF_HEAD_DIM)
    q = jnp.dot(u, w_ref[:, 0:width], preferred_element_type=F32)
    q_ref[...] = (q * qk_scale).astype(BF16)
    k = jnp.dot(u, w_ref[:, width:2 * width], preferred_element_type=F32)
    lane = lax.broadcasted_iota(jnp.int32, k.shape, 1)
    first = (lane & DIFF_HEAD_DIM) == 0
    k1_ref[...] = jnp.where(first, k, 0.0).astype(BF16)
    k2_ref[...] = jnp.where(first, 0.0, k).astype(BF16)
    v_ref[...] = jnp.dot(u, w_ref[:, 2 * width:3 * width], preferred_element_type=F32).astype(BF16)
    s_ref[...] = jnp.dot(u, w_ref[:, 3 * width:4 * width], preferred_element_type=F32)


def _inproj(x2d, shift, scale, w_in, *, seq):
    n, d = x2d.shape
    width = w_in.shape[1] // 4
    tm = min(PROJ_TOKEN_TILE, seq)
    tiles_per_seq = seq // tm
    mod_spec = pl.BlockSpec((1, 1, d), lambda i: (i // tiles_per_seq, 0, 0))
    out_spec = pl.BlockSpec((tm, width), lambda i: (i, 0))
    bf = jax.ShapeDtypeStruct((n, width), BF16)
    return pl.pallas_call(
        functools.partial(_inproj_kernel, width=width),
        grid=(n // tm,),
        in_specs=[pl.BlockSpec((tm, d), lambda i: (i, 0)), mod_spec, mod_spec,
                  _const_spec(w_in.shape)],
        out_specs=[out_spec] * 5,
        out_shape=[bf, bf, bf, bf, jax.ShapeDtypeStruct((n, width), F32)],
        compiler_params=_params("parallel"),
        name="inproj",
    )(x2d, shift, scale, w_in)


def _attn_kernel(q_ref, k1_ref, k2_ref, v_ref, lq1_ref, lk1_ref, lq2_ref, lk2_ref, g_ref,
                 wa_ref, wb_ref, wc_ref, o_ref, wa_out, wb_out, wc_out,
                 vt_scr, sa_scr, sb_scr, pma_scr, pmb_scr, p_scr, m_scr, acc_scr,
                 *, tq, tk, lambda_init):
    qi = pl.program_id(2)
    nt = (((1,), (1,)), ((), ()))
    k_refs = (k1_ref, k2_ref)
    _cast_blocks((wa_ref, wb_ref, wc_ref), (wa_out, wb_out, wc_out))
    n_kv = vt_scr.shape[0]
    v_rows = vt_scr.shape[1]

    @pl.when(qi == 0)
    def _():
        for kb in range(n_kv):
            vt = jnp.transpose(v_ref[kb * tk:(kb + 1) * tk, :].astype(F32))
            vt_scr[kb, 0:LANES, :] = vt.astype(BF16)
            vt_scr[kb, LANES:v_rows, :] = jnp.ones((v_rows - LANES, tk), BF16)

    m_scr[...] = jnp.full(m_scr.shape, NEG_BIG, F32)
    acc_scr[...] = jnp.zeros(acc_scr.shape, F32)

    def row_max(s):
        return jnp.max(s.reshape(tk // 8, 8, tq), axis=0)

    def scores(kb, s_scr, pm_scr):
        rows_k = pl.ds(pl.multiple_of(kb * tk, tk), tk)
        q = q_ref[...]
        for mp in range(2):
            s = lax.dot_general(k_refs[mp][rows_k, :], q, nt, preferred_element_type=F32)
            s_scr[mp] = s
            pm_scr[mp] = row_max(s)

    def consume(kb, s_scr, pm_scr, masked):
        vt = vt_scr[kb]
        for mp in range(2):
            if masked:
                r = lax.broadcasted_iota(jnp.int32, (tk, tq), 0)
                c = lax.broadcasted_iota(jnp.int32, (tk, tq), 1)
                s = jnp.where(r - c <= qi * tq - kb * tk, s_scr[mp], -jnp.inf)
                s_scr[mp] = s
                pm = row_max(s)
            else:
                pm = pm_scr[mp]
            m_old = m_scr[mp]
            m_new = jnp.maximum(m_old, jnp.broadcast_to(jnp.max(pm, axis=0, keepdims=True), (8, tq)))
            m_scr[mp] = m_new
            corr = jnp.exp2(m_old - m_new)
            m_rows = jnp.concatenate([m_new] * (ATTN_ROW_STRIP // 8), axis=0)
            for r0 in range(0, tk, ATTN_ROW_STRIP):
                rs = pl.ds(r0, ATTN_ROW_STRIP)
                p_scr[mp, rs, :] = jnp.exp2(s_scr[mp, rs, :] - m_rows).astype(BF16)
            res = jnp.dot(vt, p_scr[mp], preferred_element_type=F32)
            acc_scr[mp] = jnp.concatenate([corr] * (v_rows // 8), axis=0) * acc_scr[mp] + res

    scores(0, sa_scr, pma_scr)

    def pair(kb):
        scores(kb + 1, sb_scr, pmb_scr)
        consume(kb, sa_scr, pma_scr, False)
        scores(kb + 2, sa_scr, pma_scr)
        consume(kb + 1, sb_scr, pmb_scr, False)

    def body(j, c):
        pair(4 * j)
        pair(4 * j + 2)
        return c
    lax.fori_loop(0, qi // 2, body, 0)

    @pl.when(qi % 2 == 1)
    def _():
        pair(2 * qi - 2)

    consume(2 * qi, sa_scr, pma_scr, True)

    kb = 2 * qi + 1
    rows_k = pl.ds(pl.multiple_of(kb * tk, tk), tk)
    q_late = q_ref[tk:, :]
    vt = vt_scr[kb]
    r = lax.broadcasted_iota(jnp.int32, (tk, tk), 0)
    c = lax.broadcasted_iota(jnp.int32, (tk, tk), 1)
    for mp in range(2):
        s = lax.dot_general(k_refs[mp][rows_k, :], q_late, nt, preferred_element_type=F32)
        s = jnp.where(r <= c, s, -jnp.inf)
        sb_scr[mp, :, 0:tk] = s
        m_old = m_scr[mp, :, tk:]
        col_max = jnp.max(jnp.max(s.reshape(tk // 8, 8, tk), axis=0), axis=0, keepdims=True)
        m_new = jnp.maximum(m_old, jnp.broadcast_to(col_max, (8, tk)))
        m_scr[mp, :, tk:] = m_new
        corr = jnp.exp2(m_old - m_new)
        m_rows = jnp.concatenate([m_new] * (ATTN_ROW_STRIP // 8), axis=0)
        for r0 in range(0, tk, ATTN_ROW_STRIP):
            rs = pl.ds(r0, ATTN_ROW_STRIP)
            p_scr[mp, rs, 0:tk] = jnp.exp2(sb_scr[mp, rs, 0:tk] - m_rows).astype(BF16)
        res = jnp.dot(vt, p_scr[mp, :, 0:tk], preferred_element_type=F32)
        acc_scr[mp, :, tk:] = (jnp.concatenate([corr] * (v_rows // 8), axis=0) * acc_scr[mp, :, tk:] + res)

    lam = (jnp.exp(jnp.sum(lq1_ref[...] * lk1_ref[...], axis=-1, keepdims=True))
           - jnp.exp(jnp.sum(lq2_ref[...] * lk2_ref[...], axis=-1, keepdims=True)) + lambda_init)

    def normalised(mp):
        denom = jnp.concatenate([acc_scr[mp, LANES:LANES + 8, :]] * (LANES // 8), axis=0)
        return acc_scr[mp, 0:LANES, :] / denom

    o = jnp.transpose(normalised(0) - lam * normalised(1))
    o = o * lax.rsqrt(jnp.mean(o * o, axis=-1, keepdims=True) + RMS_EPS)
    o_ref[...] = (o * g_ref[...] * (1.0 - lambda_init)).astype(BF16)


def _attention(q, k1, k2, v, lq1, lk1, lq2, lk2, subln_g, later_weights, *, batch, seq, lambda_init):
    n, width = q.shape
    heads = width // LANES
    tk = min(ATTN_KV_TILE, seq // 2)
    tq = 2 * tk
    nq = seq // tq
    v_rows = LANES + 16
    kv_spec = pl.BlockSpec((seq, LANES), lambda b, h, i: (b, h))
    lam_spec = pl.BlockSpec((1, DIFF_HEAD_DIM), lambda b, h, i: (0, 0))
    q_spec = pl.BlockSpec((tq, LANES), lambda b, h, i: (b * nq + i, h))
    kern = functools.partial(_attn_kernel, tq=tq, tk=tk, lambda_init=lambda_init)
    row = lambda a: a.reshape(1, -1).astype(F32)
    score_buf = pltpu.VMEM((2, tk, tq), F32)
    stat_buf = pltpu.VMEM((2, 8, tq), F32)
    cast_specs, cast_shapes = _cast_specs(later_weights, batch * heads * nq,
                                          lambda b, h, i: (b * heads + h) * nq + i)
    out, *casts = pl.pallas_call(
        kern,
        grid=(batch, heads, nq),
        in_specs=[q_spec, kv_spec, kv_spec, kv_spec, lam_spec, lam_spec, lam_spec, lam_spec,
                  pl.BlockSpec((1, LANES), lambda b, h, i: (0, 0))] + cast_specs,
        out_specs=[q_spec] + cast_specs,
        out_shape=[jax.ShapeDtypeStruct((n, width), BF16)] + cast_shapes,
        scratch_shapes=[pltpu.VMEM((seq // tk, v_rows, tk), BF16), score_buf, score_buf, stat_buf, stat_buf,
                        pltpu.VMEM((2, tk, tq), BF16), stat_buf, pltpu.VMEM((2, v_rows, tq), F32)],
        compiler_params=_params("parallel", "parallel", "arbitrary"),
        name="attn",
    )(q, k1, k2, v, row(lq1), row(lk1), row(lq2), row(lk2), row(subln_g), *later_weights)
    return out, casts


def _ssm_weights(a_re, a_im, log_dt, b_re, b_im, c_re, c_im):
    g_n, p_n, h_n = b_re.shape
    gpb = SSM_COL_TILE // h_n
    nb = g_n // gpb
    dt = jnp.exp(log_dt)[:, None]
    mag = jnp.exp(a_re * dt)
    lbar_re = mag * jnp.cos(a_im * dt)
    lbar_im = mag * jnp.sin(a_im * dt)
    nr, ni = lbar_re - 1.0, lbar_im
    den = jnp.square(a_re) + jnp.square(a_im)
    coef_re = ((nr * a_re + ni * a_im) / den)[..., None]
    coef_im = ((ni * a_re - nr * a_im) / den)[..., None]
    bb_re = coef_re * b_re - coef_im * b_im
    bb_im = coef_re * b_im + coef_im * b_re
    eye = jnp.eye(gpb, dtype=F32)

    def in_diag(bb):
        m = bb.reshape(nb, gpb, p_n, h_n).transpose(0, 1, 3, 2)
        return (m[:, :, :, None, :] * eye[None, :, None, :, None]).reshape(nb, gpb * h_n, gpb * p_n)

    def out_diag(c):
        m = c.reshape(nb, gpb, h_n, p_n).transpose(0, 1, 3, 2)
        return (m[:, :, :, None, :] * eye[None, :, None, :, None]).reshape(nb, gpb * p_n, gpb * h_n)

    lam = lambda v: v.reshape(nb, 1, gpb * p_n)
    return (in_diag(bb_re).astype(BF16), in_diag(bb_im).astype(BF16),
            out_diag(c_re).astype(BF16), out_diag(-c_im).astype(BF16), lam(lbar_re), lam(lbar_im))


def _ssm_kernel(s_ref, bre_ref, bim_ref, cre_ref, cim_ref, lre_ref, lim_ref, w_ref, y_ref, w_out,
                slab_scr, sp_scr, xre, xim, loc_re, loc_im, ent_re, ent_im, hre, him, st_re, st_im,
                *, tt, t_len):
    nchunk = tt // t_len
    n_slab = s_ref.shape[1] // LANES
    width = xre.shape[1]
    _cast_blocks((w_ref,), (w_out,))

    @pl.when(pl.program_id(2) == 0)
    def _():
        st_re[...] = jnp.zeros(st_re.shape, F32)
        st_im[...] = jnp.zeros(st_im.shape, F32)

    for k in range(n_slab):
        slab_scr[k] = s_ref[:, k * LANES:(k + 1) * LANES]
    for k in range(n_slab):
        for r in range(t_len):
            sp_scr[r * nchunk:(r + 1) * nchunk, k * LANES:(k + 1) * LANES] = (
                slab_scr[k, pl.ds(r, nchunk, stride=t_len), :].astype(BF16))
    sp = sp_scr[...]
    xre[...] = jnp.dot(sp, bre_ref[0], preferred_element_type=F32)
    xim[...] = jnp.dot(sp, bim_ref[0], preferred_element_type=F32)

    def advance(lr, li, hr, hi, rows, lanes):
        return (lr * hr - li * hi + xre[rows, lanes], lr * hi + li * hr + xim[rows, lanes])

    for lb in range(width // LANES):
        lanes = slice(lb * LANES, (lb + 1) * LANES)
        lr, li = lre_ref[0, :, lanes], lim_ref[0, :, lanes]
        hr, hi = xre[0:nchunk, lanes], xim[0:nchunk, lanes]
        for r in range(1, t_len):
            hr, hi = advance(lr, li, hr, hi, slice(r * nchunk, (r + 1) * nchunk), lanes)
        loc_re[:, lanes] = hr
        loc_im[:, lanes] = hi

    dr, di = lre_ref[0], lim_ref[0]
    for _ in range(t_len.bit_length() - 1):
        dr, di = dr * dr - di * di, 2.0 * dr * di

    def chunk_step(c, carry):
        sr, si = carry
        row = pl.ds(c, 1)
        ent_re[row, :] = sr
        ent_im[row, :] = si
        return (dr * sr - di * si + loc_re[row, :], dr * si + di * sr + loc_im[row, :])

    sr, si = lax.fori_loop(0, nchunk, chunk_step, (st_re[...], st_im[...]))
    st_re[...] = sr
    st_im[...] = si

    for lb in range(width // LANES):
        lanes = slice(lb * LANES, (lb + 1) * LANES)
        lr, li = lre_ref[0, :, lanes], lim_ref[0, :, lanes]
        hr, hi = ent_re[:, lanes], ent_im[:, lanes]
        for r in range(t_len):
            rows = slice(r * nchunk, (r + 1) * nchunk)
            hr, hi = advance(lr, li, hr, hi, rows, lanes)
            hre[rows, lanes] = hr.astype(BF16)
            him[rows, lanes] = hi.astype(BF16)

    yp = (jnp.dot(hre[...], cre_ref[0], preferred_element_type=F32)
          + jnp.dot(him[...], cim_ref[0], preferred_element_type=F32))
    for k in range(n_slab):
        for r in range(t_len):
            slab_scr[k, pl.ds(r, nchunk, stride=t_len), :] = yp[r * nchunk:(r + 1) * nchunk,
                                                              k * LANES:(k + 1) * LANES]
        y_ref[:, k * LANES:(k + 1) * LANES] = slab_scr[k]


def _ssm(s2d, wts, later_weight, *, batch, seq):
    bre, bim, cre, cim, lre, lim = wts
    n, width = s2d.shape
    tt = min(SSM_TOKEN_TILE, seq)
    t_len = SSM_CHUNK
    assert t_len & (t_len - 1) == 0 and tt % t_len == 0 and seq % tt == 0
    tiles = seq // tt
    nchunk = tt // t_len
    col = SSM_COL_TILE
    state_w = bre.shape[2]
    tok_spec = pl.BlockSpec((tt, col), lambda b, j, i: (b * tiles + i, j))
    wspec = lambda shape: pl.BlockSpec((1,) + shape, lambda b, j, i: (j, 0, 0))
    kern = functools.partial(_ssm_kernel, tt=tt, t_len=t_len)
    n_col = width // col
    cast_specs, cast_shapes = _cast_specs((later_weight,), batch * n_col * tiles,
                                          lambda b, j, i: (b * n_col + j) * tiles + i)
    return pl.pallas_call(
        kern,
        grid=(batch, n_col, tiles),
        in_specs=[tok_spec, wspec((col, state_w)), wspec((col, state_w)),
                  wspec((state_w, col)), wspec((state_w, col)), wspec((1, state_w)), wspec((1, state_w))]
        + cast_specs,
        out_specs=[tok_spec] + cast_specs,
        out_shape=[jax.ShapeDtypeStruct((n, width), F32)] + cast_shapes,
        scratch_shapes=[pltpu.VMEM((col // LANES, tt, LANES), F32), pltpu.VMEM((tt, col), BF16),
                        pltpu.VMEM((tt, state_w), F32), pltpu.VMEM((tt, state_w), F32),
                        pltpu.VMEM((nchunk, state_w), F32), pltpu.VMEM((nchunk, state_w), F32),
                        pltpu.VMEM((nchunk, state_w), F32), pltpu.VMEM((nchunk, state_w), F32),
                        pltpu.VMEM((tt, state_w), BF16), pltpu.VMEM((tt, state_w), BF16),
                        pltpu.VMEM((1, state_w), F32), pltpu.VMEM((1, state_w), F32)],
        compiler_params=_params("parallel", "parallel", "arbitrary"),
        name="ssm",
    )(s2d, bre, bim, cre, cim, lre, lim, later_weight)


def _gelu_tanh(x):
    return 0.5 * x * (1.0 + jnp.tanh(math.sqrt(2.0 / math.pi) * (x + 0.044715 * (x * x * x))))


def _mixout_kernel(attn_ref, y_ref, s_ref, x_ref, gt_ref, dskip_ref, gluw_ref, glub_ref, wout_ref,
                   g_ref, b_ref, w_ref, o_ref, w_out, *, alpha, width):
    _cast_blocks((w_ref,), (w_out,))
    y = y_ref[...] + dskip_ref[...] * s_ref[...]
    ge = _gelu_tanh(y)
    z = jnp.dot(ge.astype(BF16), gluw_ref[...], preferred_element_type=F32) + glub_ref[...]
    ssm = (ge * jax.nn.sigmoid(z)).astype(BF16)
    h = (jnp.dot(attn_ref[...], wout_ref[0:width, :], preferred_element_type=F32)
         + jnp.dot(ssm, wout_ref[width:2 * width, :], preferred_element_type=F32))
    out = alpha * x_ref[...] + (1.0 + gt_ref[0]) * h
    o_ref[...] = _layer_norm_rows(out, g_ref[...], b_ref[...])


def _mixout(attn, y, s2d, x2d, gate, d_skip, glu_w, glu_b, w_out, ln_g, ln_b, later_weight, *, seq, alpha):
    n, d = x2d.shape
    width = attn.shape[1]
    tm = min(PROJ_TOKEN_TILE, seq)
    tiles_per_seq = seq // tm
    half = pl.BlockSpec((tm, width), lambda i: (i, 0))
    full = pl.BlockSpec((tm, d), lambda i: (i, 0))
    cast_specs, cast_shapes = _cast_specs((later_weight,), n // tm, lambda i: i)
    return pl.pallas_call(
        functools.partial(_mixout_kernel, alpha=alpha, width=width),
        grid=(n // tm,),
        in_specs=[half, half, half, full,
                  pl.BlockSpec((1, 1, d), lambda i: (i // tiles_per_seq, 0, 0)),
                  _const_spec((1, width)), _const_spec(glu_w.shape), _const_spec((1, width)),
                  _const_spec(w_out.shape), _const_spec((1, d)), _const_spec((1, d))] + cast_specs,
        out_specs=[full] + cast_specs,
        out_shape=[jax.ShapeDtypeStruct((n, d), F32)] + cast_shapes,
        compiler_params=_params("parallel"),
        name="mixout",
    )(attn, y, s2d, x2d, gate, d_skip.reshape(1, width), glu_w, glu_b.reshape(1, width), w_out,
      ln_g.reshape(1, d), ln_b.reshape(1, d), later_weight)


def kernel(x, c, w_cond, b_cond, ffn1_w1, ffn1_w3, ffn1_w2, w_in, lambda_q1, lambda_k1, lambda_q2, lambda_k2, subln_g, ssm_a_re, ssm_a_im, ssm_log_dt, ssm_b_re, ssm_b_im, ssm_c_re, ssm_c_im, ssm_d, glu_w, glu_b, w_out, ffn2_w1, ffn2_w3, ffn2_w2, ln_g, ln_b):
    batch, seq, d = x.shape
    depth = w_cond.shape[0]
    alpha = (2 * depth) ** 0.25
    h = x.reshape(batch * seq, d)
    for i in range(depth):
        lambda_init = 0.8 - 0.6 * math.exp(-0.3 * i)
        mod = _cond(c, w_cond[i], b_cond[i]).reshape(batch, N_SUBLAYERS, 3, 1, d)
        shift, scale, gate = (lambda k: mod[:, k, 0]), (lambda k: mod[:, k, 1]), (lambda k: mod[:, k, 2])

        h = _ffn(h, shift(0), scale(0), gate(0), ffn1_w1[i].astype(BF16), ffn1_w3[i].astype(BF16),
                 ffn1_w2[i].astype(BF16), ln_g[i, 0], ln_b[i, 0], seq=seq, alpha=alpha)

        q, k1, k2, v, s = _inproj(h, shift(1), scale(1), w_in[i].astype(BF16), seq=seq)
        attn, (w1b, w3b, woutb) = _attention(
            q, k1, k2, v, lambda_q1[i], lambda_k1[i], lambda_q2[i], lambda_k2[i], subln_g[i],
            (ffn2_w1[i], ffn2_w3[i], w_out[i]), batch=batch, seq=seq, lambda_init=lambda_init)
        wts = _ssm_weights(ssm_a_re[i], ssm_a_im[i], ssm_log_dt[i], ssm_b_re[i], ssm_b_im[i],
                           ssm_c_re[i], ssm_c_im[i])
        y, gluwb = _ssm(s, wts, glu_w[i], batch=batch, seq=seq)
        h, w2b = _mixout(attn, y, s, h, gate(1), ssm_d[i], gluwb, glu_b[i], woutb,
                         ln_g[i, 1], ln_b[i, 1], ffn2_w2[i], seq=seq, alpha=alpha)

        h = _ffn(h, shift(2), scale(2), gate(2), w1b, w3b, w2b, ln_g[i, 2], ln_b[i, 2], seq=seq, alpha=alpha)
    return h.reshape(batch, seq, d)
```

```python
import functools
import math

import jax
import jax.numpy as jnp
from jax import lax
from jax.experimental import pallas as pl
from jax.experimental.pallas import tpu as pltpu

F32 = jnp.float32
BF16 = jnp.bfloat16

N_SUBLAYERS = 3
DIFF_HEAD_DIM = 64
SSM_GROUP = 16
SSM_STATE = 64
LN_EPS = 1e-5
RMS_EPS = 1e-5
FFN_RES_WEIGHT = 0.5
SSM_CHUNK = 16
SSM_TOKEN_TILE = 1024
SSM_COL_TILE = 256

V7X_VMEM_BYTES = 64 * 1024 * 1024
VMEM_LIMIT = V7X_VMEM_BYTES - 4 * 1024 * 1024
LANES = 128

FFN_TOKEN_TILE = 1024
FFN_HIDDEN_TILE = 512
FFN_OUT_CHUNK = 512
ROW_CHUNK = 256
PROJ_TOKEN_TILE = 512
ATTN_KV_TILE = 512
ATTN_ROW_STRIP = 64
LOG2E = 1.4426950408889634
COND_COL_TILE = 1152
NEG_BIG = -1e30


def _params(*sem):
    return pltpu.CompilerParams(dimension_semantics=sem, vmem_limit_bytes=VMEM_LIMIT)


def _const_spec(shape):
    nd = len(shape)
    return pl.BlockSpec(shape, lambda *_: (0,) * nd, pipeline_mode=pl.Buffered(1))


def _cast_specs(weights, n_steps, step_index):
    specs, shapes = [], []
    for w in weights:
        rows, cols = w.shape
        assert rows % (16 * n_steps) == 0, (w.shape, n_steps)
        specs.append(pl.BlockSpec((rows // n_steps, cols), lambda *idx: (step_index(*idx), 0)))
        shapes.append(jax.ShapeDtypeStruct(w.shape, BF16))
    return specs, shapes


def _cast_blocks(srcs, dsts):
    for src, dst in zip(srcs, dsts):
        dst[...] = src[...].astype(BF16)


def _layer_norm_rows(y, g, b):
    mu = jnp.mean(y, axis=-1, keepdims=True)
    yc = y - mu
    var = jnp.mean(yc * yc, axis=-1, keepdims=True)
    return yc * lax.rsqrt(var + LN_EPS) * g + b


def _cond_kernel(c_ref, w_ref, b_ref, wa_ref, wb_ref, wc_ref, wd_ref, o_ref, wa_out, wb_out, wc_out, wd_out):
    _cast_blocks((wa_ref, wb_ref, wc_ref, wd_ref), (wa_out, wb_out, wc_out, wd_out))
    c = c_ref[...]
    a = c * jax.nn.sigmoid(c)
    o_ref[...] = jnp.dot(a, w_ref[...], preferred_element_type=F32,
                         precision=lax.Precision.HIGHEST) + b_ref[...]


def _cond(c, w_cond, b_cond, later_weights):
    bsz, d = c.shape
    ncol = w_cond.shape[1]
    rows = 8
    c_pad = jnp.zeros((rows, d), F32).at[:bsz].set(c)
    tn = COND_COL_TILE
    steps = ncol // tn
    cast_specs, cast_shapes = _cast_specs(later_weights, steps, lambda j: j)
    out, *casts = pl.pallas_call(
        _cond_kernel,
        grid=(steps,),
        in_specs=[pl.BlockSpec((rows, d), lambda j: (0, 0)),
                  pl.BlockSpec((d, tn), lambda j: (0, j)),
                  pl.BlockSpec((1, tn), lambda j: (0, j))] + cast_specs,
        out_specs=[pl.BlockSpec((rows, tn), lambda j: (0, j))] + cast_specs,
        out_shape=[jax.ShapeDtypeStruct((rows, ncol), F32)] + cast_shapes,
        compiler_params=_params("arbitrary"),
        name="cond",
    )(c_pad, w_cond, b_cond.reshape(1, ncol), *later_weights)
    return out[:bsz], casts


def _ffn_kernel(x_ref, sh_ref, sc_ref, gt_ref, w1_ref, w3_ref, w2_ref, g_ref, b_ref,
                o_ref, u_ref, *, alpha, n_hidden_tiles, tm, d):
    j = pl.program_id(1)

    @pl.when(j == 0)
    def _():
        for r in range(0, tm, ROW_CHUNK):
            rows = pl.ds(r, ROW_CHUNK)
            u_ref[rows, :] = (x_ref[rows, :] * (1.0 + sc_ref[0]) + sh_ref[0]).astype(BF16)

    u = u_ref[...]
    h1 = jnp.dot(u, w1_ref[...], preferred_element_type=F32)
    h3 = jnp.dot(u, w3_ref[...], preferred_element_type=F32)
    a = (h1 * jax.nn.sigmoid(h1) * h3).astype(BF16)

    @pl.when(j == 0)
    def _():
        for n in range(0, d, FFN_OUT_CHUNK):
            cols = pl.ds(n, FFN_OUT_CHUNK)
            o_ref[:, cols] = jnp.dot(a, w2_ref[:, cols], preferred_element_type=F32)

    @pl.when(j > 0)
    def _():
        for n in range(0, d, FFN_OUT_CHUNK):
            cols = pl.ds(n, FFN_OUT_CHUNK)
            o_ref[:, cols] += jnp.dot(a, w2_ref[:, cols], preferred_element_type=F32)

    @pl.when(j == n_hidden_tiles - 1)
    def _():
        gate = FFN_RES_WEIGHT * (1.0 + gt_ref[0])
        for r in range(0, tm, ROW_CHUNK):
            rows = pl.ds(r, ROW_CHUNK)
            y = alpha * x_ref[rows, :] + gate * o_ref[rows, :]
            o_ref[rows, :] = _layer_norm_rows(y, g_ref[...], b_ref[...])


def _ffn(x2d, shift, scale, gate, w1, w3, w2, ln_g, ln_b, *, seq, alpha):
    n, d = x2d.shape
    f = w1.shape[1]
    tm = min(FFN_TOKEN_TILE, seq)
    tf = FFN_HIDDEN_TILE
    tiles_per_seq = seq // tm
    nj = f // tf
    mod_spec = pl.BlockSpec((1, 1, d), lambda i, j: (i // tiles_per_seq, 0, 0))
    vec_spec = pl.BlockSpec((1, d), lambda i, j: (0, 0))
    kern = functools.partial(_ffn_kernel, alpha=alpha, n_hidden_tiles=nj, tm=tm, d=d)
    return pl.pallas_call(
        kern,
        grid=(n // tm, nj),
        in_specs=[pl.BlockSpec((tm, d), lambda i, j: (i, 0)),
                  mod_spec, mod_spec, mod_spec,
                  pl.BlockSpec((d, tf), lambda i, j: (0, j)),
                  pl.BlockSpec((d, tf), lambda i, j: (0, j)),
                  pl.BlockSpec((tf, d), lambda i, j: (j, 0)),
                  vec_spec, vec_spec],
        out_specs=pl.BlockSpec((tm, d), lambda i, j: (i, 0)),
        out_shape=jax.ShapeDtypeStruct((n, d), F32),
        scratch_shapes=[pltpu.VMEM((tm, d), BF16)],
        compiler_params=_params("parallel", "arbitrary"),
        name="ffn",
    )(x2d, shift, scale, gate, w1, w3, w2, ln_g.reshape(1, d), ln_b.reshape(1, d))


def _inproj_kernel(x_ref, sh_ref, sc_ref, w_ref, q_ref, k1_ref, k2_ref, v_ref, s_ref, *, width):
    u = (x_ref[...] * (1.0 + sc_ref[0]) + sh_ref[0]).astype(BF16)
    qk_scale = LOG2E / math.sqrt(DIFF_HEAD_DIM)
    q = jnp.dot(u, w_ref[:, 0:width], preferred_element_type=F32)
    q_ref[...] = (q * qk_scale).astype(BF16)
    k = jnp.dot(u, w_ref[:, width:2 * width], preferred_element_type=F32)
    lane = lax.broadcasted_iota(jnp.int32, k.shape, 1)
    first = (lane & DIFF_HEAD_DIM) == 0
    k1_ref[...] = jnp.where(first, k, 0.0).astype(BF16)
    k2_ref[...] = jnp.where(first, 0.0, k).astype(BF16)
    v_ref[...] = jnp.dot(u, w_ref[:, 2 * width:3 * width], preferred_element_type=F32).astype(BF16)
    s_ref[...] = jnp.dot(u, w_ref[:, 3 * width:4 * width], preferred_element_type=F32)


def _inproj(x2d, shift, scale, w_in, *, seq):
    n, d = x2d.shape
    width = w_in.shape[1] // 4
    tm = min(PROJ_TOKEN_TILE, seq)
    tiles_per_seq = seq // tm
    mod_spec = pl.BlockSpec((1, 1, d), lambda i: (i // tiles_per_seq, 0, 0))
    out_spec = pl.BlockSpec((tm, width), lambda i: (i, 0))
    bf = jax.ShapeDtypeStruct((n, width), BF16)
    return pl.pallas_call(
        functools.partial(_inproj_kernel, width=width),
        grid=(n // tm,),
        in_specs=[pl.BlockSpec((tm, d), lambda i: (i, 0)), mod_spec, mod_spec,
                  _const_spec(w_in.shape)],
        out_specs=[out_spec] * 5,
        out_shape=[bf, bf, bf, bf, jax.ShapeDtypeStruct((n, width), F32)],
        compiler_params=_params("parallel"),
        name="inproj",
    )(x2d, shift, scale, w_in)


def _attn_kernel(q_ref, k1_ref, k2_ref, v_ref, lq1_ref, lk1_ref, lq2_ref, lk2_ref, g_ref,
                 wa_ref, wb_ref, wc_ref, o_ref, wa_out, wb_out, wc_out,
                 vt_scr, sa_scr, sb_scr, pma_scr, pmb_scr, p_scr, m_scr, acc_scr,
                 *, tq, tk, lambda_init):
    qi = pl.program_id(2)
    nt = (((1,), (1,)), ((), ()))
    k_refs = (k1_ref, k2_ref)
    _cast_blocks((wa_ref, wb_ref, wc_ref), (wa_out, wb_out, wc_out))
    n_kv = vt_scr.shape[0]
    v_rows = vt_scr.shape[1]

    @pl.when(qi == 0)
    def _():
        for kb in range(n_kv):
            vt = jnp.transpose(v_ref[kb * tk:(kb + 1) * tk, :].astype(F32))
            vt_scr[kb, 0:LANES, :] = vt.astype(BF16)
            vt_scr[kb, LANES:v_rows, :] = jnp.ones((v_rows - LANES, tk), BF16)

    m_scr[...] = jnp.full(m_scr.shape, NEG_BIG, F32)
    acc_scr[...] = jnp.zeros(acc_scr.shape, F32)

    def row_max(s):
        return jnp.max(s.reshape(tk // 8, 8, tq), axis=0)

    def scores(kb, s_scr, pm_scr):
        rows_k = pl.ds(pl.multiple_of(kb * tk, tk), tk)
        q = q_ref[...]
        for mp in range(2):
            s = lax.dot_general(k_refs[mp][rows_k, :], q, nt, preferred_element_type=F32)
            s_scr[mp] = s
            pm_scr[mp] = row_max(s)

    def consume(kb, s_scr, pm_scr, masked):
        vt = vt_scr[kb]
        for mp in range(2):
            if masked:
                r = lax.broadcasted_iota(jnp.int32, (tk, tq), 0)
                c = lax.broadcasted_iota(jnp.int32, (tk, tq), 1)
                s = jnp.where(r - c <= qi * tq - kb * tk, s_scr[mp], -jnp.inf)
                s_scr[mp] = s
                pm = row_max(s)
            else:
                pm = pm_scr[mp]
            m_old = m_scr[mp]
            m_new = jnp.maximum(m_old, jnp.broadcast_to(jnp.max(pm, axis=0, keepdims=True), (8, tq)))
            m_scr[mp] = m_new
            corr = jnp.exp2(m_old - m_new)
            m_rows = jnp.concatenate([m_new] * (ATTN_ROW_STRIP // 8), axis=0)
            for r0 in range(0, tk, ATTN_ROW_STRIP):
                rs = pl.ds(r0, ATTN_ROW_STRIP)
                p_scr[mp, rs, :] = jnp.exp2(s_scr[mp, rs, :] - m_rows).astype(BF16)
            res = jnp.dot(vt, p_scr[mp], preferred_element_type=F32)
            acc_scr[mp] = jnp.concatenate([corr] * (v_rows // 8), axis=0) * acc_scr[mp] + res

    scores(0, sa_scr, pma_scr)

    def pair(kb):
        scores(kb + 1, sb_scr, pmb_scr)
        consume(kb, sa_scr, pma_scr, False)
        scores(kb + 2, sa_scr, pma_scr)
        consume(kb + 1, sb_scr, pmb_scr, False)

    def body(j, c):
        pair(4 * j)
        pair(4 * j + 2)
        return c
    lax.fori_loop(0, qi // 2, body, 0)

    @pl.when(qi % 2 == 1)
    def _():
        pair(2 * qi - 2)

    consume(2 * qi, sa_scr, pma_scr, True)

    kb = 2 * qi + 1
    rows_k = pl.ds(pl.multiple_of(kb * tk, tk), tk)
    q_late = q_ref[tk:, :]
    vt = vt_scr[kb]
    r = lax.broadcasted_iota(jnp.int32, (tk, tk), 0)
    c = lax.broadcasted_iota(jnp.int32, (tk, tk), 1)
    for mp in range(2):
        s = lax.dot_general(k_refs[mp][rows_k, :], q_late, nt, preferred_element_type=F32)
        s = jnp.where(r <= c, s, -jnp.inf)
        sb_scr[mp, :, 0:tk] = s
        m_old = m_scr[mp, :, tk:]
        col_max = jnp.max(jnp.max(s.reshape(tk // 8, 8, tk), axis=0), axis=0, keepdims=True)
        m_new = jnp.maximum(m_old, jnp.broadcast_to(col_max, (8, tk)))
        m_scr[mp, :, tk:] = m_new
        corr = jnp.exp2(m_old - m_new)
        m_rows = jnp.concatenate([m_new] * (ATTN_ROW_STRIP // 8), axis=0)
        for r0 in range(0, tk, ATTN_ROW_STRIP):
            rs = pl.ds(r0, ATTN_ROW_STRIP)
            p_scr[mp, rs, 0:tk] = jnp.exp2(sb_scr[mp, rs, 0:tk] - m_rows).astype(BF16)
        res = jnp.dot(vt, p_scr[mp, :, 0:tk], preferred_element_type=F32)
        acc_scr[mp, :, tk:] = (jnp.concatenate([corr] * (v_rows // 8), axis=0) * acc_scr[mp, :, tk:] + res)

    lam = (jnp.exp(jnp.sum(lq1_ref[...] * lk1_ref[...], axis=-1, keepdims=True))
           - jnp.exp(jnp.sum(lq2_ref[...] * lk2_ref[...], axis=-1, keepdims=True)) + lambda_init)

    def normalised(mp):
        denom = jnp.concatenate([acc_scr[mp, LANES:LANES + 8, :]] * (LANES // 8), axis=0)
        return acc_scr[mp, 0:LANES, :] / denom

    o = jnp.transpose(normalised(0) - lam * normalised(1))
    o = o * lax.rsqrt(jnp.mean(o * o, axis=-1, keepdims=True) + RMS_EPS)
    o_ref[...] = (o * g_ref[...] * (1.0 - lambda_init)).astype(BF16)


def _attention(q, k1, k2, v, lq1, lk1, lq2, lk2, subln_g, later_weights, *, batch, seq, lambda_init):
    n, width = q.shape
    heads = width // LANES
    tk = min(ATTN_KV_TILE, seq // 2)
    tq = 2 * tk
    nq = seq // tq
    v_rows = LANES + 16
    kv_spec = pl.BlockSpec((seq, LANES), lambda b, h, i: (b, h))
    lam_spec = pl.BlockSpec((1, DIFF_HEAD_DIM), lambda b, h, i: (0, 0))
    q_spec = pl.BlockSpec((tq, LANES), lambda b, h, i: (b * nq + i, h))
    kern = functools.partial(_attn_kernel, tq=tq, tk=tk, lambda_init=lambda_init)
    row = lambda a: a.reshape(1, -1).astype(F32)
    score_buf = pltpu.VMEM((2, tk, tq), F32)
    stat_buf = pltpu.VMEM((2, 8, tq), F32)
    cast_specs, cast_shapes = _cast_specs(later_weights, batch * heads * nq,
                                          lambda b, h, i: (b * heads + h) * nq + i)
    out, *casts = pl.pallas_call(
        kern,
        grid=(batch, heads, nq),
        in_specs=[q_spec, kv_spec, kv_spec, kv_spec, lam_spec, lam_spec, lam_spec, lam_spec,
                  pl.BlockSpec((1, LANES), lambda b, h, i: (0, 0))] + cast_specs,
        out_specs=[q_spec] + cast_specs,
        out_shape=[jax.ShapeDtypeStruct((n, width), BF16)] + cast_shapes,
        scratch_shapes=[pltpu.VMEM((seq // tk, v_rows, tk), BF16), score_buf, score_buf, stat_buf, stat_buf,
                        pltpu.VMEM((2, tk, tq), BF16), stat_buf, pltpu.VMEM((2, v_rows, tq), F32)],
        compiler_params=_params("parallel", "parallel", "arbitrary"),
        name="attn",
    )(q, k1, k2, v, row(lq1), row(lk1), row(lq2), row(lk2), row(subln_g), *later_weights)
    return out, casts


def _ssm_weights(a_re, a_im, log_dt, b_re, b_im, c_re, c_im):
    g_n, p_n, h_n = b_re.shape
    gpb = SSM_COL_TILE // h_n
    nb = g_n // gpb
    dt = jnp.exp(log_dt)[:, None]
    mag = jnp.exp(a_re * dt)
    lbar_re = mag * jnp.cos(a_im * dt)
    lbar_im = mag * jnp.sin(a_im * dt)
    nr, ni = lbar_re - 1.0, lbar_im
    den = jnp.square(a_re) + jnp.square(a_im)
    coef_re = ((nr * a_re + ni * a_im) / den)[..., None]
    coef_im = ((ni * a_re - nr * a_im) / den)[..., None]
    bb_re = coef_re * b_re - coef_im * b_im
    bb_im = coef_re * b_im + coef_im * b_re
    eye = jnp.eye(gpb, dtype=F32)

    def in_diag(bb):
        m = bb.reshape(nb, gpb, p_n, h_n).transpose(0, 1, 3, 2)
        return (m[:, :, :, None, :] * eye[None, :, None, :, None]).reshape(nb, gpb * h_n, gpb * p_n)

    def out_diag(c):
        m = c.reshape(nb, gpb, h_n, p_n).transpose(0, 1, 3, 2)
        return (m[:, :, :, None, :] * eye[None, :, None, :, None]).reshape(nb, gpb * p_n, gpb * h_n)

    lam = lambda v: v.reshape(nb, 1, gpb * p_n)
    return (in_diag(bb_re).astype(BF16), in_diag(bb_im).astype(BF16),
            out_diag(c_re).astype(BF16), out_diag(-c_im).astype(BF16), lam(lbar_re), lam(lbar_im))


def _ssm_kernel(s_ref, sn_ref, bre_ref, bim_ref, cre_ref, cim_ref, lre_ref, lim_ref, w_ref, y_ref, w_out,
                slab_in, slab_out, sp_scr, xa_re, xa_im, xb_re, xb_im, loc_re, loc_im, ent_re, ent_im,
                hre, him, st_re, st_im, *, tt, t_len):
    i = pl.program_id(2)
    nchunk = tt // t_len
    n_slab = s_ref.shape[1] // LANES
    width = xa_re.shape[1]
    col_tile = 2 * LANES
    _cast_blocks((w_ref,), (w_out,))

    def project(src_ref, xre, xim):
        for k in range(n_slab):
            slab_in[k] = src_ref[:, k * LANES:(k + 1) * LANES]
        for k in range(n_slab):
            for r in range(t_len):
                sp_scr[r * nchunk:(r + 1) * nchunk, k * LANES:(k + 1) * LANES] = (
                    slab_in[k, pl.ds(r, nchunk, stride=t_len), :].astype(BF16))
        sp = sp_scr[...]
        xre[...] = jnp.dot(sp, bre_ref[0], preferred_element_type=F32)
        xim[...] = jnp.dot(sp, bim_ref[0], preferred_element_type=F32)

    def recur(xre, xim):
        def advance(lr, li, hr, hi, rows, lanes):
            return (lr * hr - li * hi + xre[rows, lanes], lr * hi + li * hr + xim[rows, lanes])

        for lb in range(width // LANES):
            lanes = slice(lb * LANES, (lb + 1) * LANES)
            lr, li = lre_ref[0, :, lanes], lim_ref[0, :, lanes]
            hr, hi = xre[0:nchunk, lanes], xim[0:nchunk, lanes]
            for r in range(1, t_len):
                hr, hi = advance(lr, li, hr, hi, slice(r * nchunk, (r + 1) * nchunk), lanes)
            loc_re[:, lanes] = hr
            loc_im[:, lanes] = hi

        dr, di = lre_ref[0], lim_ref[0]
        for _ in range(t_len.bit_length() - 1):
            dr, di = dr * dr - di * di, 2.0 * dr * di

        def chunk_step(c, carry):
            sr, si = carry
            row = pl.ds(c, 1)
            ent_re[row, :] = sr
            ent_im[row, :] = si
            return (dr * sr - di * si + loc_re[row, :], dr * si + di * sr + loc_im[row, :])

        sr, si = lax.fori_loop(0, nchunk, chunk_step, (st_re[...], st_im[...]), unroll=True)
        st_re[...] = sr
        st_im[...] = si

        yp = None
        for n in range(width // col_tile):
            for lb in range(n * col_tile // LANES, (n + 1) * col_tile // LANES):
                lanes = slice(lb * LANES, (lb + 1) * LANES)
                lr, li = lre_ref[0, :, lanes], lim_ref[0, :, lanes]
                hr, hi = ent_re[:, lanes], ent_im[:, lanes]
                for r in range(t_len):
                    rows = slice(r * nchunk, (r + 1) * nchunk)
                    hr, hi = advance(lr, li, hr, hi, rows, lanes)
                    hre[rows, lanes] = hr.astype(BF16)
                    him[rows, lanes] = hi.astype(BF16)
            cols = slice(n * col_tile, (n + 1) * col_tile)
            part = (jnp.dot(hre[:, cols], cre_ref[0, cols, :], preferred_element_type=F32)
                    + jnp.dot(him[:, cols], cim_ref[0, cols, :], preferred_element_type=F32))
            yp = part if yp is None else yp + part
        for k in range(n_slab):
            for r in range(t_len):
                slab_out[k, pl.ds(r, nchunk, stride=t_len), :] = yp[r * nchunk:(r + 1) * nchunk,
                                                                  k * LANES:(k + 1) * LANES]
            y_ref[:, k * LANES:(k + 1) * LANES] = slab_out[k]

    @pl.when(i == 0)
    def _():
        st_re[...] = jnp.zeros(st_re.shape, F32)
        st_im[...] = jnp.zeros(st_im.shape, F32)
        project(s_ref, xa_re, xa_im)

    @pl.when(i % 2 == 0)
    def _():
        project(sn_ref, xb_re, xb_im)
        recur(xa_re, xa_im)

    @pl.when(i % 2 == 1)
    def _():
        project(sn_ref, xa_re, xa_im)
        recur(xb_re, xb_im)


def _ssm(s2d, wts, later_weight, *, batch, seq):
    bre, bim, cre, cim, lre, lim = wts
    n, width = s2d.shape
    tt = min(SSM_TOKEN_TILE, seq)
    t_len = SSM_CHUNK
    assert t_len & (t_len - 1) == 0 and tt % t_len == 0 and seq % tt == 0
    tiles = seq // tt
    nchunk = tt // t_len
    col = SSM_COL_TILE
    state_w = bre.shape[2]
    tok_spec = pl.BlockSpec((tt, col), lambda b, j, i: (b * tiles + i, j))
    next_spec = pl.BlockSpec((tt, col), lambda b, j, i: (b * tiles + jnp.minimum(i + 1, tiles - 1), j))
    wspec = lambda shape: pl.BlockSpec((1,) + shape, lambda b, j, i: (j, 0, 0))
    kern = functools.partial(_ssm_kernel, tt=tt, t_len=t_len)
    n_col = width // col
    cast_specs, cast_shapes = _cast_specs((later_weight,), batch * n_col * tiles,
                                          lambda b, j, i: (b * n_col + j) * tiles + i)
    return pl.pallas_call(
        kern,
        grid=(batch, n_col, tiles),
        in_specs=[tok_spec, next_spec, wspec((col, state_w)), wspec((col, state_w)),
                  wspec((state_w, col)), wspec((state_w, col)), wspec((1, state_w)), wspec((1, state_w))]
        + cast_specs,
        out_specs=[tok_spec] + cast_specs,
        out_shape=[jax.ShapeDtypeStruct((n, width), F32)] + cast_shapes,
        scratch_shapes=[pltpu.VMEM((col // LANES, tt, LANES), F32), pltpu.VMEM((col // LANES, tt, LANES), F32),
                        pltpu.VMEM((tt, col), BF16),
                        pltpu.VMEM((tt, state_w), F32), pltpu.VMEM((tt, state_w), F32),
                        pltpu.VMEM((tt, state_w), F32), pltpu.VMEM((tt, state_w), F32),
                        pltpu.VMEM((nchunk, state_w), F32), pltpu.VMEM((nchunk, state_w), F32),
                        pltpu.VMEM((nchunk, state_w), F32), pltpu.VMEM((nchunk, state_w), F32),
                        pltpu.VMEM((tt, state_w), BF16), pltpu.VMEM((tt, state_w), BF16),
                        pltpu.VMEM((1, state_w), F32), pltpu.VMEM((1, state_w), F32)],
        compiler_params=_params("parallel", "parallel", "arbitrary"),
        name="ssm",
    )(s2d, s2d, bre, bim, cre, cim, lre, lim, later_weight)


def _gelu_tanh(x):
    return 0.5 * x * (1.0 + jnp.tanh(math.sqrt(2.0 / math.pi) * (x + 0.044715 * (x * x * x))))


def _mixout_kernel(attn_ref, y_ref, s_ref, x_ref, gt_ref, dskip_ref, gluw_ref, glub_ref, wout_ref,
                   g_ref, b_ref, w_ref, o_ref, w_out, *, alpha, width):
    _cast_blocks((w_ref,), (w_out,))
    y = y_ref[...] + dskip_ref[...] * s_ref[...]
    ge = _gelu_tanh(y)
    z = jnp.dot(ge.astype(BF16), gluw_ref[...], preferred_element_type=F32) + glub_ref[...]
    ssm = (ge * jax.nn.sigmoid(z)).astype(BF16)
    h = (jnp.dot(attn_ref[...], wout_ref[0:width, :], preferred_element_type=F32)
         + jnp.dot(ssm, wout_ref[width:2 * width, :], preferred_element_type=F32))
    out = alpha * x_ref[...] + (1.0 + gt_ref[0]) * h
    o_ref[...] = _layer_norm_rows(out, g_ref[...], b_ref[...])


def _mixout(attn, y, s2d, x2d, gate, d_skip, glu_w, glu_b, w_out, ln_g, ln_b, later_weight, *, seq, alpha):
    n, d = x2d.shape
    width = attn.shape[1]
    tm = min(PROJ_TOKEN_TILE, seq)
    tiles_per_seq = seq // tm
    half = pl.BlockSpec((tm, width), lambda i: (i, 0))
    full = pl.BlockSpec((tm, d), lambda i: (i, 0))
    cast_specs, cast_shapes = _cast_specs((later_weight,), n // tm, lambda i: i)
    return pl.pallas_call(
        functools.partial(_mixout_kernel, alpha=alpha, width=width),
        grid=(n // tm,),
        in_specs=[half, half, half, full,
                  pl.BlockSpec((1, 1, d), lambda i: (i // tiles_per_seq, 0, 0)),
                  _const_spec((1, width)), _const_spec(glu_w.shape), _const_spec((1, width)),
                  _const_spec(w_out.shape), _const_spec((1, d)), _const_spec((1, d))] + cast_specs,
        out_specs=[full] + cast_specs,
        out_shape=[jax.ShapeDtypeStruct((n, d), F32)] + cast_shapes,
        compiler_params=_params("parallel"),
        name="mixout",
    )(attn, y, s2d, x2d, gate, d_skip.reshape(1, width), glu_w, glu_b.reshape(1, width), w_out,
      ln_g.reshape(1, d), ln_b.reshape(1, d), later_weight)


def kernel(x, c, w_cond, b_cond, ffn1_w1, ffn1_w3, ffn1_w2, w_in, lambda_q1, lambda_k1, lambda_q2, lambda_k2, subln_g, ssm_a_re, ssm_a_im, ssm_log_dt, ssm_b_re, ssm_b_im, ssm_c_re, ssm_c_im, ssm_d, glu_w, glu_b, w_out, ffn2_w1, ffn2_w3, ffn2_w2, ln_g, ln_b):
    batch, seq, d = x.shape
    depth = w_cond.shape[0]
    alpha = (2 * depth) ** 0.25
    h = x.reshape(batch * seq, d)
    for i in range(depth):
        lambda_init = 0.8 - 0.6 * math.exp(-0.3 * i)
        mod, (f1w1, f1w3, f1w2, w_inb) = _cond(c, w_cond[i], b_cond[i],
                                               (ffn1_w1[i], ffn1_w3[i], ffn1_w2[i], w_in[i]))
        mod = mod.reshape(batch, N_SUBLAYERS, 3, 1, d)
        shift, scale, gate = (lambda k: mod[:, k, 0]), (lambda k: mod[:, k, 1]), (lambda k: mod[:, k, 2])

        h = _ffn(h, shift(0), scale(0), gate(0), f1w1, f1w3, f1w2, ln_g[i, 0], ln_b[i, 0], seq=seq, alpha=alpha)

        q, k1, k2, v, s = _inproj(h, shift(1), scale(1), w_inb, seq=seq)
        attn, (w1b, w3b, woutb) = _attention(
            q, k1, k2, v, lambda_q1[i], lambda_k1[i], lambda_q2[i], lambda_k2[i], subln_g[i],
            (ffn2_w1[i], ffn2_w3[i], w_out[i]), batch=batch, seq=seq, lambda_init=lambda_init)
        wts = _ssm_weights(ssm_a_re[i], ssm_a_im[i], ssm_log_dt[i], ssm_b_re[i], ssm_b_im[i],
                           ssm_c_re[i], ssm_c_im[i])
        y, gluwb = _ssm(s, wts, glu_w[i], batch=batch, seq=seq)
        h, w2b = _mixout(attn, y, s, h, gate(1), ssm_d[i], gluwb, glu_b[i], woutb,
                         ln_g[i, 1], ln_b[i, 1], ffn2_w2[i], seq=seq, alpha=alpha)

        h = _ffn(h, shift(2), scale(2), gate(2), w1b, w3b, w2b, ln_g[i, 2], ln_b[i, 2], seq=seq, alpha=alpha)
    return h.reshape(batch, seq, d)
```

```python
import functools
import math

import jax
import jax.numpy as jnp
from jax import lax
from jax.experimental import pallas as pl
from jax.experimental.pallas import tpu as pltpu

F32 = jnp.float32
BF16 = jnp.bfloat16

N_SUBLAYERS = 3
DIFF_HEAD_DIM = 64
SSM_GROUP = 16
SSM_STATE = 64
LN_EPS = 1e-5
RMS_EPS = 1e-5
FFN_RES_WEIGHT = 0.5
SSM_CHUNK = 16
SSM_TOKEN_TILE = 1024
SSM_COL_TILE = 256

V7X_VMEM_BYTES = 64 * 1024 * 1024
VMEM_LIMIT = V7X_VMEM_BYTES - 4 * 1024 * 1024
LANES = 128

FFN_TOKEN_TILE = 1024
FFN_HIDDEN_TILE = 512
FFN_OUT_CHUNK = 512
ROW_CHUNK = 256
PROJ_TOKEN_TILE = 512
ATTN_KV_TILE = 512
ATTN_ROW_STRIP = 64
LOG2E = 1.4426950408889634
COND_COL_TILE = 1152
NEG_BIG = -1e30


def _params(*sem):
    return pltpu.CompilerParams(dimension_semantics=sem, vmem_limit_bytes=VMEM_LIMIT)


def _const_spec(shape):
    nd = len(shape)
    return pl.BlockSpec(shape, lambda *_: (0,) * nd, pipeline_mode=pl.Buffered(1))


def _cast_specs(weights, n_steps, step_index):
    specs, shapes = [], []
    for w in weights:
        rows, cols = w.shape
        assert rows % (16 * n_steps) == 0, (w.shape, n_steps)
        specs.append(pl.BlockSpec((rows // n_steps, cols), lambda *idx: (step_index(*idx), 0)))
        shapes.append(jax.ShapeDtypeStruct(w.shape, BF16))
    return specs, shapes


def _cast_blocks(srcs, dsts):
    for src, dst in zip(srcs, dsts):
        dst[...] = src[...].astype(BF16)


def _layer_norm_rows(y, g, b):
    mu = jnp.mean(y, axis=-1, keepdims=True)
    yc = y - mu
    var = jnp.mean(yc * yc, axis=-1, keepdims=True)
    return yc * lax.rsqrt(var + LN_EPS) * g + b


def _cond_kernel(c_ref, w_ref, b_ref, wa_ref, wb_ref, wc_ref, wd_ref, o_ref, wa_out, wb_out, wc_out, wd_out):
    _cast_blocks((wa_ref, wb_ref, wc_ref, wd_ref), (wa_out, wb_out, wc_out, wd_out))
    c = c_ref[...]
    a = c * jax.nn.sigmoid(c)
    o_ref[...] = jnp.dot(a, w_ref[...], preferred_element_type=F32,
                         precision=lax.Precision.HIGHEST) + b_ref[...]


def _cond(c, w_cond, b_cond, later_weights):
    bsz, d = c.shape
    ncol = w_cond.shape[1]
    rows = 8
    c_pad = jnp.zeros((rows, d), F32).at[:bsz].set(c)
    tn = COND_COL_TILE
    steps = ncol // tn
    cast_specs, cast_shapes = _cast_specs(later_weights, steps, lambda j: j)
    out, *casts = pl.pallas_call(
        _cond_kernel,
        grid=(steps,),
        in_specs=[pl.BlockSpec((rows, d), lambda j: (0, 0)),
                  pl.BlockSpec((d, tn), lambda j: (0, j)),
                  pl.BlockSpec((1, tn), lambda j: (0, j))] + cast_specs,
        out_specs=[pl.BlockSpec((rows, tn), lambda j: (0, j))] + cast_specs,
        out_shape=[jax.ShapeDtypeStruct((rows, ncol), F32)] + cast_shapes,
        compiler_params=_params("arbitrary"),
        name="cond",
    )(c_pad, w_cond, b_cond.reshape(1, ncol), *later_weights)
    return out[:bsz], casts


def _ffn_kernel(x_ref, sh_ref, sc_ref, gt_ref, w1_ref, w3_ref, w2_ref, g_ref, b_ref,
                o_ref, u_ref, *, alpha, n_hidden_tiles, tm, d):
    j = pl.program_id(1)

    @pl.when(j == 0)
    def _():
        for r in range(0, tm, ROW_CHUNK):
            rows = pl.ds(r, ROW_CHUNK)
            u_ref[rows, :] = (x_ref[rows, :] * (1.0 + sc_ref[0]) + sh_ref[0]).astype(BF16)

    u = u_ref[...]
    h1 = jnp.dot(u, w1_ref[...], preferred_element_type=F32)
    h3 = jnp.dot(u, w3_ref[...], preferred_element_type=F32)
    a = (h1 * jax.nn.sigmoid(h1) * h3).astype(BF16)

    @pl.when(j == 0)
    def _():
        for n in range(0, d, FFN_OUT_CHUNK):
            cols = pl.ds(n, FFN_OUT_CHUNK)
            o_ref[:, cols] = jnp.dot(a, w2_ref[:, cols], preferred_element_type=F32)

    @pl.when(j > 0)
    def _():
        for n in range(0, d, FFN_OUT_CHUNK):
            cols = pl.ds(n, FFN_OUT_CHUNK)
            o_ref[:, cols] += jnp.dot(a, w2_ref[:, cols], preferred_element_type=F32)

    @pl.when(j == n_hidden_tiles - 1)
    def _():
        gate = FFN_RES_WEIGHT * (1.0 + gt_ref[0])
        for r in range(0, tm, ROW_CHUNK):
            rows = pl.ds(r, ROW_CHUNK)
            y = alpha * x_ref[rows, :] + gate * o_ref[rows, :]
            o_ref[rows, :] = _layer_norm_rows(y, g_ref[...], b_ref[...])


def _ffn(x2d, shift, scale, gate, w1, w3, w2, ln_g, ln_b, *, seq, alpha):
    n, d = x2d.shape
    f = w1.shape[1]
    tm = min(FFN_TOKEN_TILE, seq)
    tf = FFN_HIDDEN_TILE
    tiles_per_seq = seq // tm
    nj = f // tf
    mod_spec = pl.BlockSpec((1, 1, d), lambda i, j: (i // tiles_per_seq, 0, 0))
    vec_spec = pl.BlockSpec((1, d), lambda i, j: (0, 0))
    kern = functools.partial(_ffn_kernel, alpha=alpha, n_hidden_tiles=nj, tm=tm, d=d)
    return pl.pallas_call(
        kern,
        grid=(n // tm, nj),
        in_specs=[pl.BlockSpec((tm, d), lambda i, j: (i, 0)),
                  mod_spec, mod_spec, mod_spec,
                  pl.BlockSpec((d, tf), lambda i, j: (0, j)),
                  pl.BlockSpec((d, tf), lambda i, j: (0, j)),
                  pl.BlockSpec((tf, d), lambda i, j: (j, 0)),
                  vec_spec, vec_spec],
        out_specs=pl.BlockSpec((tm, d), lambda i, j: (i, 0)),
        out_shape=jax.ShapeDtypeStruct((n, d), F32),
        scratch_shapes=[pltpu.VMEM((tm, d), BF16)],
        compiler_params=_params("parallel", "arbitrary"),
        name="ffn",
    )(x2d, shift, scale, gate, w1, w3, w2, ln_g.reshape(1, d), ln_b.reshape(1, d))


def _inproj_kernel(x_ref, sh_ref, sc_ref, w_ref, q_ref, k1_ref, k2_ref, v_ref, s_ref, *, width):
    u = (x_ref[...] * (1.0 + sc_ref[0]) + sh_ref[0]).astype(BF16)
    qk_scale = LOG2E / math.sqrt(DIFF_HEAD_DIM)
    q = jnp.dot(u, w_ref[:, 0:width], preferred_element_type=F32)
    q_ref[...] = (q * qk_scale).astype(BF16)
    k = jnp.dot(u, w_ref[:, width:2 * width], preferred_element_type=F32)
    lane = lax.broadcasted_iota(jnp.int32, k.shape, 1)
    first = (lane & DIFF_HEAD_DIM) == 0
    k1_ref[...] = jnp.where(first, k, 0.0).astype(BF16)
    k2_ref[...] = jnp.where(first, 0.0, k).astype(BF16)
    v_ref[...] = jnp.dot(u, w_ref[:, 2 * width:3 * width], preferred_element_type=F32).astype(BF16)
    s_ref[...] = jnp.dot(u, w_ref[:, 3 * width:4 * width], preferred_element_type=F32)


def _inproj(x2d, shift, scale, w_in, *, seq):
    n, d = x2d.shape
    width = w_in.shape[1] // 4
    tm = min(PROJ_TOKEN_TILE, seq)
    tiles_per_seq = seq // tm
    mod_spec = pl.BlockSpec((1, 1, d), lambda i: (i // tiles_per_seq, 0, 0))
    out_spec = pl.BlockSpec((tm, width), lambda i: (i, 0))
    bf = jax.ShapeDtypeStruct((n, width), BF16)
    return pl.pallas_call(
        functools.partial(_inproj_kernel, width=width),
        grid=(n // tm,),
        in_specs=[pl.BlockSpec((tm, d), lambda i: (i, 0)), mod_spec, mod_spec,
                  _const_spec(w_in.shape)],
        out_specs=[out_spec] * 5,
        out_shape=[bf, bf, bf, bf, jax.ShapeDtypeStruct((n, width), F32)],
        compiler_params=_params("parallel"),
        name="inproj",
    )(x2d, shift, scale, w_in)


def _attn_kernel(q_ref, k1_ref, k2_ref, v_ref, lq1_ref, lk1_ref, lq2_ref, lk2_ref, g_ref,
                 wa_ref, wb_ref, wc_ref, o_ref, wa_out, wb_out, wc_out,
                 vt_scr, sa_scr, sb_scr, pma_scr, pmb_scr, p_scr, m_scr, acc_scr,
                 *, tq, tk, lambda_init):
    qi = pl.program_id(2)
    nt = (((1,), (1,)), ((), ()))
    k_refs = (k1_ref, k2_ref)
    _cast_blocks((wa_ref, wb_ref, wc_ref), (wa_out, wb_out, wc_out))
    n_kv = vt_scr.shape[0]
    v_rows = vt_scr.shape[1]

    @pl.when(qi == 0)
    def _():
        for kb in range(n_kv):
            vt = jnp.transpose(v_ref[kb * tk:(kb + 1) * tk, :].astype(F32))
            vt_scr[kb, 0:LANES, :] = vt.astype(BF16)
            vt_scr[kb, LANES:v_rows, :] = jnp.ones((v_rows - LANES, tk), BF16)

    m_scr[...] = jnp.full(m_scr.shape, NEG_BIG, F32)
    acc_scr[...] = jnp.zeros(acc_scr.shape, F32)

    def row_max(s):
        return jnp.max(s.reshape(tk // 8, 8, tq), axis=0)

    def scores(kb, s_scr, pm_scr):
        rows_k = pl.ds(pl.multiple_of(kb * tk, tk), tk)
        q = q_ref[...]
        for mp in range(2):
            s = lax.dot_general(k_refs[mp][rows_k, :], q, nt, preferred_element_type=F32)
            s_scr[mp] = s
            pm_scr[mp] = row_max(s)

    def consume(kb, s_scr, pm_scr, masked):
        vt = vt_scr[kb]
        for mp in range(2):
            if masked:
                r = lax.broadcasted_iota(jnp.int32, (tk, tq), 0)
                c = lax.broadcasted_iota(jnp.int32, (tk, tq), 1)
                s = jnp.where(r - c <= qi * tq - kb * tk, s_scr[mp], -jnp.inf)
                s_scr[mp] = s
                pm = row_max(s)
            else:
                pm = pm_scr[mp]
            m_old = m_scr[mp]
            m_new = jnp.maximum(m_old, jnp.broadcast_to(jnp.max(pm, axis=0, keepdims=True), (8, tq)))
            m_scr[mp] = m_new
            corr = jnp.exp2(m_old - m_new)
            m_rows = jnp.concatenate([m_new] * (ATTN_ROW_STRIP // 8), axis=0)
            for r0 in range(0, tk, ATTN_ROW_STRIP):
                rs = pl.ds(r0, ATTN_ROW_STRIP)
                p_scr[mp, rs, :] = jnp.exp2(s_scr[mp, rs, :] - m_rows).astype(BF16)
            res = jnp.dot(vt, p_scr[mp], preferred_element_type=F32)
            acc_scr[mp] = jnp.concatenate([corr] * (v_rows // 8), axis=0) * acc_scr[mp] + res

    scores(0, sa_scr, pma_scr)

    def pair(kb):
        scores(kb + 1, sb_scr, pmb_scr)
        consume(kb, sa_scr, pma_scr, False)
        scores(kb + 2, sa_scr, pma_scr)
        consume(kb + 1, sb_scr, pmb_scr, False)

    def body(j, c):
        pair(4 * j)
        pair(4 * j + 2)
        return c
    lax.fori_loop(0, qi // 2, body, 0)

    @pl.when(qi % 2 == 1)
    def _():
        pair(2 * qi - 2)

    consume(2 * qi, sa_scr, pma_scr, True)

    kb = 2 * qi + 1
    rows_k = pl.ds(pl.multiple_of(kb * tk, tk), tk)
    q_late = q_ref[tk:, :]
    vt = vt_scr[kb]
    r = lax.broadcasted_iota(jnp.int32, (tk, tk), 0)
    c = lax.broadcasted_iota(jnp.int32, (tk, tk), 1)
    for mp in range(2):
        s = lax.dot_general(k_refs[mp][rows_k, :], q_late, nt, preferred_element_type=F32)
        s = jnp.where(r <= c, s, -jnp.inf)
        sb_scr[mp, :, 0:tk] = s
        m_old = m_scr[mp, :, tk:]
        col_max = jnp.max(jnp.max(s.reshape(tk // 8, 8, tk), axis=0), axis=0, keepdims=True)
        m_new = jnp.maximum(m_old, jnp.broadcast_to(col_max, (8, tk)))
        m_scr[mp, :, tk:] = m_new
        corr = jnp.exp2(m_old - m_new)
        m_rows = jnp.concatenate([m_new] * (ATTN_ROW_STRIP // 8), axis=0)
        for r0 in range(0, tk, ATTN_ROW_STRIP):
            rs = pl.ds(r0, ATTN_ROW_STRIP)
            p_scr[mp, rs, 0:tk] = jnp.exp2(sb_scr[mp, rs, 0:tk] - m_rows).astype(BF16)
        res = jnp.dot(vt, p_scr[mp, :, 0:tk], preferred_element_type=F32)
        acc_scr[mp, :, tk:] = (jnp.concatenate([corr] * (v_rows // 8), axis=0) * acc_scr[mp, :, tk:] + res)

    lam = (jnp.exp(jnp.sum(lq1_ref[...] * lk1_ref[...], axis=-1, keepdims=True))
           - jnp.exp(jnp.sum(lq2_ref[...] * lk2_ref[...], axis=-1, keepdims=True)) + lambda_init)

    def normalised(mp):
        denom = jnp.concatenate([acc_scr[mp, LANES:LANES + 8, :]] * (LANES // 8), axis=0)
        return acc_scr[mp, 0:LANES, :] / denom

    o = jnp.transpose(normalised(0) - lam * normalised(1))
    o = o * lax.rsqrt(jnp.mean(o * o, axis=-1, keepdims=True) + RMS_EPS)
    o_ref[...] = (o * g_ref[...] * (1.0 - lambda_init)).astype(BF16)


def _attention(q, k1, k2, v, lq1, lk1, lq2, lk2, subln_g, later_weights, *, batch, seq, lambda_init):
    n, width = q.shape
    heads = width // LANES
    tk = min(ATTN_KV_TILE, seq // 2)
    tq = 2 * tk
    nq = seq // tq
    v_rows = LANES + 16
    kv_spec = pl.BlockSpec((seq, LANES), lambda b, h, i: (b, h))
    lam_spec = pl.BlockSpec((1, DIFF_HEAD_DIM), lambda b, h, i: (0, 0))
    q_spec = pl.BlockSpec((tq, LANES), lambda b, h, i: (b * nq + i, h))
    kern = functools.partial(_attn_kernel, tq=tq, tk=tk, lambda_init=lambda_init)
    row = lambda a: a.reshape(1, -1).astype(F32)
    score_buf = pltpu.VMEM((2, tk, tq), F32)
    stat_buf = pltpu.VMEM((2, 8, tq), F32)
    cast_specs, cast_shapes = _cast_specs(later_weights, batch * heads * nq,
                                          lambda b, h, i: (b * heads + h) * nq + i)
    out, *casts = pl.pallas_call(
        kern,
        grid=(batch, heads, nq),
        in_specs=[q_spec, kv_spec, kv_spec, kv_spec, lam_spec, lam_spec, lam_spec, lam_spec,
                  pl.BlockSpec((1, LANES), lambda b, h, i: (0, 0))] + cast_specs,
        out_specs=[q_spec] + cast_specs,
        out_shape=[jax.ShapeDtypeStruct((n, width), BF16)] + cast_shapes,
        scratch_shapes=[pltpu.VMEM((seq // tk, v_rows, tk), BF16), score_buf, score_buf, stat_buf, stat_buf,
                        pltpu.VMEM((2, tk, tq), BF16), stat_buf, pltpu.VMEM((2, v_rows, tq), F32)],
        compiler_params=_params("parallel", "parallel", "arbitrary"),
        name="attn",
    )(q, k1, k2, v, row(lq1), row(lk1), row(lq2), row(lk2), row(subln_g), *later_weights)
    return out, casts


def _ssm_weights(a_re, a_im, log_dt, b_re, b_im, c_re, c_im):
    g_n, p_n, h_n = b_re.shape
    gpb = SSM_COL_TILE // h_n
    nb = g_n // gpb
    dt = jnp.exp(log_dt)[:, None]
    mag = jnp.exp(a_re * dt)
    lbar_re = mag * jnp.cos(a_im * dt)
    lbar_im = mag * jnp.sin(a_im * dt)
    nr, ni = lbar_re - 1.0, lbar_im
    den = jnp.square(a_re) + jnp.square(a_im)
    coef_re = ((nr * a_re + ni * a_im) / den)[..., None]
    coef_im = ((ni * a_re - nr * a_im) / den)[..., None]
    bb_re = coef_re * b_re - coef_im * b_im
    bb_im = coef_re * b_im + coef_im * b_re
    def block_diag(compact, blk_rows, blk_cols):
        tiled = jnp.tile(compact, (1, 1, gpb))
        rows = lax.broadcasted_iota(jnp.int32, tiled.shape, 1) // blk_rows
        cols = lax.broadcasted_iota(jnp.int32, tiled.shape, 2) // blk_cols
        return jnp.where(rows == cols, tiled, 0.0).astype(BF16)

    def in_diag(bb):
        m = bb.reshape(nb, gpb, p_n, h_n).transpose(0, 1, 3, 2).reshape(nb, gpb * h_n, p_n)
        return block_diag(m, h_n, p_n)

    def out_diag(c):
        m = c.reshape(nb, gpb, h_n, p_n).transpose(0, 1, 3, 2).reshape(nb, gpb * p_n, h_n)
        return block_diag(m, p_n, h_n)

    lam = lambda v: v.reshape(nb, 1, gpb * p_n)
    return (in_diag(bb_re), in_diag(bb_im), out_diag(c_re), out_diag(-c_im), lam(lbar_re), lam(lbar_im))


def _ssm_kernel(s_ref, sn_ref, bre_ref, bim_ref, cre_ref, cim_ref, lre_ref, lim_ref, w_ref, y_ref, w_out,
                slab_in, slab_out, sp_scr, xa_re, xa_im, xb_re, xb_im, loc_re, loc_im, ent_re, ent_im,
                hre, him, st_re, st_im, *, tt, t_len):
    i = pl.program_id(2)
    nchunk = tt // t_len
    n_slab = s_ref.shape[1] // LANES
    width = xa_re.shape[1]
    col_tile = 2 * LANES
    _cast_blocks((w_ref,), (w_out,))

    def project(src_ref, xre, xim):
        for k in range(n_slab):
            slab_in[k] = src_ref[:, k * LANES:(k + 1) * LANES]
        for k in range(n_slab):
            for r in range(t_len):
                sp_scr[r * nchunk:(r + 1) * nchunk, k * LANES:(k + 1) * LANES] = (
                    slab_in[k, pl.ds(r, nchunk, stride=t_len), :].astype(BF16))
        sp = sp_scr[...]
        xre[...] = jnp.dot(sp, bre_ref[0], preferred_element_type=F32)
        xim[...] = jnp.dot(sp, bim_ref[0], preferred_element_type=F32)

    def recur(xre, xim):
        def advance(lr, li, hr, hi, rows, lanes):
            return (lr * hr - li * hi + xre[rows, lanes], lr * hi + li * hr + xim[rows, lanes])

        for lb in range(width // LANES):
            lanes = slice(lb * LANES, (lb + 1) * LANES)
            lr, li = lre_ref[0, :, lanes], lim_ref[0, :, lanes]
            hr, hi = xre[0:nchunk, lanes], xim[0:nchunk, lanes]
            for r in range(1, t_len):
                hr, hi = advance(lr, li, hr, hi, slice(r * nchunk, (r + 1) * nchunk), lanes)
            loc_re[:, lanes] = hr
            loc_im[:, lanes] = hi

        dr, di = lre_ref[0], lim_ref[0]
        for _ in range(t_len.bit_length() - 1):
            dr, di = dr * dr - di * di, 2.0 * dr * di

        def chunk_step(c, carry):
            sr, si = carry
            row = pl.ds(c, 1)
            ent_re[row, :] = sr
            ent_im[row, :] = si
            return (dr * sr - di * si + loc_re[row, :], dr * si + di * sr + loc_im[row, :])

        sr, si = lax.fori_loop(0, nchunk, chunk_step, (st_re[...], st_im[...]), unroll=True)
        st_re[...] = sr
        st_im[...] = si

        yp = None
        for n in range(width // col_tile):
            for lb in range(n * col_tile // LANES, (n + 1) * col_tile // LANES):
                lanes = slice(lb * LANES, (lb + 1) * LANES)
                lr, li = lre_ref[0, :, lanes], lim_ref[0, :, lanes]
                hr, hi = ent_re[:, lanes], ent_im[:, lanes]
                for r in range(t_len):
                    rows = slice(r * nchunk, (r + 1) * nchunk)
                    hr, hi = advance(lr, li, hr, hi, rows, lanes)
                    hre[rows, lanes] = hr.astype(BF16)
                    him[rows, lanes] = hi.astype(BF16)
            cols = slice(n * col_tile, (n + 1) * col_tile)
            part = (jnp.dot(hre[:, cols], cre_ref[0, cols, :], preferred_element_type=F32)
                    + jnp.dot(him[:, cols], cim_ref[0, cols, :], preferred_element_type=F32))
            yp = part if yp is None else yp + part
        for k in range(n_slab):
            for r in range(t_len):
                slab_out[k, pl.ds(r, nchunk, stride=t_len), :] = yp[r * nchunk:(r + 1) * nchunk,
                                                                  k * LANES:(k + 1) * LANES]
            y_ref[:, k * LANES:(k + 1) * LANES] = slab_out[k]

    @pl.when(i == 0)
    def _():
        st_re[...] = jnp.zeros(st_re.shape, F32)
        st_im[...] = jnp.zeros(st_im.shape, F32)
        project(s_ref, xa_re, xa_im)

    @pl.when(i % 2 == 0)
    def _():
        project(sn_ref, xb_re, xb_im)
        recur(xa_re, xa_im)

    @pl.when(i % 2 == 1)
    def _():
        project(sn_ref, xa_re, xa_im)
        recur(xb_re, xb_im)


def _ssm(s2d, wts, later_weight, *, batch, seq):
    bre, bim, cre, cim, lre, lim = wts
    n, width = s2d.shape
    tt = min(SSM_TOKEN_TILE, seq)
    t_len = SSM_CHUNK
    assert t_len & (t_len - 1) == 0 and tt % t_len == 0 and seq % tt == 0
    tiles = seq // tt
    nchunk = tt // t_len
    col = SSM_COL_TILE
    state_w = bre.shape[2]
    tok_spec = pl.BlockSpec((tt, col), lambda b, j, i: (b * tiles + i, j))
    next_spec = pl.BlockSpec((tt, col), lambda b, j, i: (b * tiles + jnp.minimum(i + 1, tiles - 1), j))
    wspec = lambda shape: pl.BlockSpec((1,) + shape, lambda b, j, i: (j, 0, 0))
    kern = functools.partial(_ssm_kernel, tt=tt, t_len=t_len)
    n_col = width // col
    cast_specs, cast_shapes = _cast_specs((later_weight,), batch * n_col * tiles,
                                          lambda b, j, i: (b * n_col + j) * tiles + i)
    return pl.pallas_call(
        kern,
        grid=(batch, n_col, tiles),
        in_specs=[tok_spec, next_spec, wspec((col, state_w)), wspec((col, state_w)),
                  wspec((state_w, col)), wspec((state_w, col)), wspec((1, state_w)), wspec((1, state_w))]
        + cast_specs,
        out_specs=[tok_spec] + cast_specs,
        out_shape=[jax.ShapeDtypeStruct((n, width), F32)] + cast_shapes,
        scratch_shapes=[pltpu.VMEM((col // LANES, tt, LANES), F32), pltpu.VMEM((col // LANES, tt, LANES), F32),
                        pltpu.VMEM((tt, col), BF16),
                        pltpu.VMEM((tt, state_w), F32), pltpu.VMEM((tt, state_w), F32),
                        pltpu.VMEM((tt, state_w), F32), pltpu.VMEM((tt, state_w), F32),
                        pltpu.VMEM((nchunk, state_w), F32), pltpu.VMEM((nchunk, state_w), F32),
                        pltpu.VMEM((nchunk, state_w), F32), pltpu.VMEM((nchunk, state_w), F32),
                        pltpu.VMEM((tt, state_w), BF16), pltpu.VMEM((tt, state_w), BF16),
                        pltpu.VMEM((1, state_w), F32), pltpu.VMEM((1, state_w), F32)],
        compiler_params=_params("parallel", "parallel", "arbitrary"),
        name="ssm",
    )(s2d, s2d, bre, bim, cre, cim, lre, lim, later_weight)


def _gelu_tanh(x):
    return 0.5 * x * (1.0 + jnp.tanh(math.sqrt(2.0 / math.pi) * (x + 0.044715 * (x * x * x))))


def _mixout_kernel(attn_ref, y_ref, s_ref, x_ref, gt_ref, dskip_ref, gluw_ref, glub_ref, wout_ref,
                   g_ref, b_ref, w_ref, o_ref, w_out, *, alpha, width):
    _cast_blocks((w_ref,), (w_out,))
    y = y_ref[...] + dskip_ref[...] * s_ref[...]
    ge = _gelu_tanh(y)
    z = jnp.dot(ge.astype(BF16), gluw_ref[...], preferred_element_type=F32) + glub_ref[...]
    ssm = (ge * jax.nn.sigmoid(z)).astype(BF16)
    h = (jnp.dot(attn_ref[...], wout_ref[0:width, :], preferred_element_type=F32)
         + jnp.dot(ssm, wout_ref[width:2 * width, :], preferred_element_type=F32))
    out = alpha * x_ref[...] + (1.0 + gt_ref[0]) * h
    o_ref[...] = _layer_norm_rows(out, g_ref[...], b_ref[...])


def _mixout(attn, y, s2d, x2d, gate, d_skip, glu_w, glu_b, w_out, ln_g, ln_b, later_weight, *, seq, alpha):
    n, d = x2d.shape
    width = attn.shape[1]
    tm = min(PROJ_TOKEN_TILE, seq)
    tiles_per_seq = seq // tm
    half = pl.BlockSpec((tm, width), lambda i: (i, 0))
    full = pl.BlockSpec((tm, d), lambda i: (i, 0))
    cast_specs, cast_shapes = _cast_specs((later_weight,), n // tm, lambda i: i)
    return pl.pallas_call(
        functools.partial(_mixout_kernel, alpha=alpha, width=width),
        grid=(n // tm,),
        in_specs=[half, half, half, full,
                  pl.BlockSpec((1, 1, d), lambda i: (i // tiles_per_seq, 0, 0)),
                  _const_spec((1, width)), _const_spec(glu_w.shape), _const_spec((1, width)),
                  _const_spec(w_out.shape), _const_spec((1, d)), _const_spec((1, d))] + cast_specs,
        out_specs=[full] + cast_specs,
        out_shape=[jax.ShapeDtypeStruct((n, d), F32)] + cast_shapes,
        compiler_params=_params("parallel"),
        name="mixout",
    )(attn, y, s2d, x2d, gate, d_skip.reshape(1, width), glu_w, glu_b.reshape(1, width), w_out,
      ln_g.reshape(1, d), ln_b.reshape(1, d), later_weight)


def kernel(x, c, w_cond, b_cond, ffn1_w1, ffn1_w3, ffn1_w2, w_in, lambda_q1, lambda_k1, lambda_q2, lambda_k2, subln_g, ssm_a_re, ssm_a_im, ssm_log_dt, ssm_b_re, ssm_b_im, ssm_c_re, ssm_c_im, ssm_d, glu_w, glu_b, w_out, ffn2_w1, ffn2_w3, ffn2_w2, ln_g, ln_b):
    batch, seq, d = x.shape
    depth = w_cond.shape[0]
    alpha = (2 * depth) ** 0.25
    h = x.reshape(batch * seq, d)
    for i in range(depth):
        lambda_init = 0.8 - 0.6 * math.exp(-0.3 * i)
        mod, (f1w1, f1w3, f1w2, w_inb) = _cond(c, w_cond[i], b_cond[i],
                                               (ffn1_w1[i], ffn1_w3[i], ffn1_w2[i], w_in[i]))
        mod = mod.reshape(batch, N_SUBLAYERS, 3, 1, d)
        shift, scale, gate = (lambda k: mod[:, k, 0]), (lambda k: mod[:, k, 1]), (lambda k: mod[:, k, 2])

        h = _ffn(h, shift(0), scale(0), gate(0), f1w1, f1w3, f1w2, ln_g[i, 0], ln_b[i, 0], seq=seq, alpha=alpha)

        q, k1, k2, v, s = _inproj(h, shift(1), scale(1), w_inb, seq=seq)
        attn, (w1b, w3b, woutb) = _attention(
            q, k1, k2, v, lambda_q1[i], lambda_k1[i], lambda_q2[i], lambda_k2[i], subln_g[i],
            (ffn2_w1[i], ffn2_w3[i], w_out[i]), batch=batch, seq=seq, lambda_init=lambda_init)
        wts = _ssm_weights(ssm_a_re[i], ssm_a_im[i], ssm_log_dt[i], ssm_b_re[i], ssm_b_im[i],
                           ssm_c_re[i], ssm_c_im[i])
        y, gluwb = _ssm(s, wts, glu_w[i], batch=batch, seq=seq)
        h, w2b = _mixout(attn, y, s, h, gate(1), ssm_d[i], gluwb, glu_b[i], woutb,
                         ln_g[i, 1], ln_b[i, 1], ffn2_w2[i], seq=seq, alpha=alpha)

        h = _ffn(h, shift(2), scale(2), gate(2), w1b, w3b, w2b, ln_g[i, 2], ln_b[i, 2], seq=seq, alpha=alpha)
    return h.reshape(batch, seq, d)
```

```python
import functools
import math

import jax
import jax.numpy as jnp
from jax import lax
from jax.experimental import pallas as pl
from jax.experimental.pallas import tpu as pltpu

F32 = jnp.float32
BF16 = jnp.bfloat16

N_SUBLAYERS = 3
DIFF_HEAD_DIM = 64
SSM_GROUP = 16
SSM_STATE = 64
LN_EPS = 1e-5
RMS_EPS = 1e-5
FFN_RES_WEIGHT = 0.5
SSM_CHUNK = 8
SSM_TOKEN_TILE = 1024
SSM_COL_TILE = 256

V7X_VMEM_BYTES = 64 * 1024 * 1024
VMEM_LIMIT = V7X_VMEM_BYTES - 4 * 1024 * 1024
LANES = 128

FFN_TOKEN_TILE = 1024
FFN_HIDDEN_TILE = 512
FFN_OUT_CHUNK = 512
ROW_CHUNK = 256
PROJ_TOKEN_TILE = 512
ATTN_KV_TILE = 512
ATTN_ROW_STRIP = 64
LOG2E = 1.4426950408889634
COND_COL_TILE = 1152
NEG_BIG = -1e30


def _params(*sem):
    return pltpu.CompilerParams(dimension_semantics=sem, vmem_limit_bytes=VMEM_LIMIT)


def _const_spec(shape):
    nd = len(shape)
    return pl.BlockSpec(shape, lambda *_: (0,) * nd, pipeline_mode=pl.Buffered(1))


def _cast_specs(weights, n_steps, step_index):
    specs, shapes = [], []
    for w in weights:
        rows, cols = w.shape
        assert rows % (16 * n_steps) == 0, (w.shape, n_steps)
        specs.append(pl.BlockSpec((rows // n_steps, cols), lambda *idx: (step_index(*idx), 0)))
        shapes.append(jax.ShapeDtypeStruct(w.shape, BF16))
    return specs, shapes


def _cast_blocks(srcs, dsts):
    for src, dst in zip(srcs, dsts):
        dst[...] = src[...].astype(BF16)


def _layer_norm_rows(y, g, b):
    mu = jnp.mean(y, axis=-1, keepdims=True)
    yc = y - mu
    var = jnp.mean(yc * yc, axis=-1, keepdims=True)
    return yc * lax.rsqrt(var + LN_EPS) * g + b


def _cond_kernel(c_ref, w_ref, b_ref, wa_ref, wb_ref, wc_ref, wd_ref, o_ref, wa_out, wb_out, wc_out, wd_out):
    _cast_blocks((wa_ref, wb_ref, wc_ref, wd_ref), (wa_out, wb_out, wc_out, wd_out))
    c = c_ref[...]
    a = c * jax.nn.sigmoid(c)
    o_ref[...] = jnp.dot(a, w_ref[...], preferred_element_type=F32,
                         precision=lax.Precision.HIGHEST) + b_ref[...]


def _cond(c, w_cond, b_cond, later_weights):
    bsz, d = c.shape
    ncol = w_cond.shape[1]
    rows = 8
    c_pad = jnp.zeros((rows, d), F32).at[:bsz].set(c)
    tn = COND_COL_TILE
    steps = ncol // tn
    cast_specs, cast_shapes = _cast_specs(later_weights, steps, lambda j: j)
    out, *casts = pl.pallas_call(
        _cond_kernel,
        grid=(steps,),
        in_specs=[pl.BlockSpec((rows, d), lambda j: (0, 0)),
                  pl.BlockSpec((d, tn), lambda j: (0, j)),
                  pl.BlockSpec((1, tn), lambda j: (0, j))] + cast_specs,
        out_specs=[pl.BlockSpec((rows, tn), lambda j: (0, j))] + cast_specs,
        out_shape=[jax.ShapeDtypeStruct((rows, ncol), F32)] + cast_shapes,
        compiler_params=_params("arbitrary"),
        name="cond",
    )(c_pad, w_cond, b_cond.reshape(1, ncol), *later_weights)
    return out[:bsz], casts


def _ffn_kernel(x_ref, sh_ref, sc_ref, gt_ref, w1_ref, w3_ref, w2_ref, g_ref, b_ref,
                o_ref, u_ref, *, alpha, n_hidden_tiles, tm, d):
    j = pl.program_id(1)

    @pl.when(j == 0)
    def _():
        for r in range(0, tm, ROW_CHUNK):
            rows = pl.ds(r, ROW_CHUNK)
            u_ref[rows, :] = (x_ref[rows, :] * (1.0 + sc_ref[0]) + sh_ref[0]).astype(BF16)

    u = u_ref[...]
    h1 = jnp.dot(u, w1_ref[...], preferred_element_type=F32)
    h3 = jnp.dot(u, w3_ref[...], preferred_element_type=F32)
    a = (h1 * jax.nn.sigmoid(h1) * h3).astype(BF16)

    @pl.when(j == 0)
    def _():
        for n in range(0, d, FFN_OUT_CHUNK):
            cols = pl.ds(n, FFN_OUT_CHUNK)
            o_ref[:, cols] = jnp.dot(a, w2_ref[:, cols], preferred_element_type=F32)

    @pl.when(j > 0)
    def _():
        for n in range(0, d, FFN_OUT_CHUNK):
            cols = pl.ds(n, FFN_OUT_CHUNK)
            o_ref[:, cols] += jnp.dot(a, w2_ref[:, cols], preferred_element_type=F32)

    @pl.when(j == n_hidden_tiles - 1)
    def _():
        gate = FFN_RES_WEIGHT * (1.0 + gt_ref[0])
        for r in range(0, tm, ROW_CHUNK):
            rows = pl.ds(r, ROW_CHUNK)
            y = alpha * x_ref[rows, :] + gate * o_ref[rows, :]
            o_ref[rows, :] = _layer_norm_rows(y, g_ref[...], b_ref[...])


def _ffn(x2d, shift, scale, gate, w1, w3, w2, ln_g, ln_b, *, seq, alpha):
    n, d = x2d.shape
    f = w1.shape[1]
    tm = min(FFN_TOKEN_TILE, seq)
    tf = FFN_HIDDEN_TILE
    tiles_per_seq = seq // tm
    nj = f // tf
    mod_spec = pl.BlockSpec((1, 1, d), lambda i, j: (i // tiles_per_seq, 0, 0))
    vec_spec = pl.BlockSpec((1, d), lambda i, j: (0, 0))
    kern = functools.partial(_ffn_kernel, alpha=alpha, n_hidden_tiles=nj, tm=tm, d=d)
    return pl.pallas_call(
        kern,
        grid=(n // tm, nj),
        in_specs=[pl.BlockSpec((tm, d), lambda i, j: (i, 0)),
                  mod_spec, mod_spec, mod_spec,
                  pl.BlockSpec((d, tf), lambda i, j: (0, j)),
                  pl.BlockSpec((d, tf), lambda i, j: (0, j)),
                  pl.BlockSpec((tf, d), lambda i, j: (j, 0)),
                  vec_spec, vec_spec],
        out_specs=pl.BlockSpec((tm, d), lambda i, j: (i, 0)),
        out_shape=jax.ShapeDtypeStruct((n, d), F32),
        scratch_shapes=[pltpu.VMEM((tm, d), BF16)],
        compiler_params=_params("parallel", "arbitrary"),
        name="ffn",
    )(x2d, shift, scale, gate, w1, w3, w2, ln_g.reshape(1, d), ln_b.reshape(1, d))


def _inproj_kernel(x_ref, sh_ref, sc_ref, w_ref, q_ref, k1_ref, k2_ref, v_ref, s_ref, *, width):
    u = (x_ref[...] * (1.0 + sc_ref[0]) + sh_ref[0]).astype(BF16)
    qk_scale = LOG2E / math.sqrt(DIFF_HEAD_DIM)
    q = jnp.dot(u, w_ref[:, 0:width], preferred_element_type=F32)
    q_ref[...] = (q * qk_scale).astype(BF16)
    k = jnp.dot(u, w_ref[:, width:2 * width], preferred_element_type=F32)
    lane = lax.broadcasted_iota(jnp.int32, k.shape, 1)
    first = (lane & DIFF_HEAD_DIM) == 0
    k1_ref[...] = jnp.where(first, k, 0.0).astype(BF16)
    k2_ref[...] = jnp.where(first, 0.0, k).astype(BF16)
    v_ref[...] = jnp.dot(u, w_ref[:, 2 * width:3 * width], preferred_element_type=F32).astype(BF16)
    s_ref[...] = jnp.dot(u, w_ref[:, 3 * width:4 * width], preferred_element_type=F32)


def _inproj(x2d, shift, scale, w_in, *, seq):
    n, d = x2d.shape
    width = w_in.shape[1] // 4
    tm = min(PROJ_TOKEN_TILE, seq)
    tiles_per_seq = seq // tm
    mod_spec = pl.BlockSpec((1, 1, d), lambda i: (i // tiles_per_seq, 0, 0))
    out_spec = pl.BlockSpec((tm, width), lambda i: (i, 0))
    bf = jax.ShapeDtypeStruct((n, width), BF16)
    return pl.pallas_call(
        functools.partial(_inproj_kernel, width=width),
        grid=(n // tm,),
        in_specs=[pl.BlockSpec((tm, d), lambda i: (i, 0)), mod_spec, mod_spec,
                  _const_spec(w_in.shape)],
        out_specs=[out_spec] * 5,
        out_shape=[bf, bf, bf, bf, jax.ShapeDtypeStruct((n, width), F32)],
        compiler_params=_params("parallel"),
        name="inproj",
    )(x2d, shift, scale, w_in)


def _attn_kernel(q_ref, k1_ref, k2_ref, v_ref, lq1_ref, lk1_ref, lq2_ref, lk2_ref, g_ref,
                 wa_ref, wb_ref, wc_ref, o_ref, wa_out, wb_out, wc_out,
                 vt_scr, sa_scr, sb_scr, pma_scr, pmb_scr, p_scr, m_scr, acc_scr,
                 *, tq, tk, lambda_init):
    qi = pl.program_id(2)
    nt = (((1,), (1,)), ((), ()))
    k_refs = (k1_ref, k2_ref)
    _cast_blocks((wa_ref, wb_ref, wc_ref), (wa_out, wb_out, wc_out))
    n_kv = vt_scr.shape[0]
    v_rows = vt_scr.shape[1]

    @pl.when(qi == 0)
    def _():
        for kb in range(n_kv):
            vt = jnp.transpose(v_ref[kb * tk:(kb + 1) * tk, :].astype(F32))
            vt_scr[kb, 0:LANES, :] = vt.astype(BF16)
            vt_scr[kb, LANES:v_rows, :] = jnp.ones((v_rows - LANES, tk), BF16)

    m_scr[...] = jnp.full(m_scr.shape, NEG_BIG, F32)
    acc_scr[...] = jnp.zeros(acc_scr.shape, F32)

    def row_max(s):
        return jnp.max(s.reshape(tk // 8, 8, tq), axis=0)

    def scores(kb, s_scr, pm_scr):
        rows_k = pl.ds(pl.multiple_of(kb * tk, tk), tk)
        q = q_ref[...]
        for mp in range(2):
            s = lax.dot_general(k_refs[mp][rows_k, :], q, nt, preferred_element_type=F32)
            s_scr[mp] = s
            pm_scr[mp] = row_max(s)

    def consume(kb, s_scr, pm_scr, masked):
        vt = vt_scr[kb]
        for mp in range(2):
            if masked:
                r = lax.broadcasted_iota(jnp.int32, (tk, tq), 0)
                c = lax.broadcasted_iota(jnp.int32, (tk, tq), 1)
                s = jnp.where(r - c <= qi * tq - kb * tk, s_scr[mp], -jnp.inf)
                s_scr[mp] = s
                pm = row_max(s)
            else:
                pm = pm_scr[mp]
            m_old = m_scr[mp]
            m_new = jnp.maximum(m_old, jnp.broadcast_to(jnp.max(pm, axis=0, keepdims=True), (8, tq)))
            m_scr[mp] = m_new
            corr = jnp.exp2(m_old - m_new)
            m_rows = jnp.concatenate([m_new] * (ATTN_ROW_STRIP // 8), axis=0)
            for r0 in range(0, tk, ATTN_ROW_STRIP):
                rs = pl.ds(r0, ATTN_ROW_STRIP)
                p_scr[mp, rs, :] = jnp.exp2(s_scr[mp, rs, :] - m_rows).astype(BF16)
            res = jnp.dot(vt, p_scr[mp], preferred_element_type=F32)
            acc_scr[mp] = jnp.concatenate([corr] * (v_rows // 8), axis=0) * acc_scr[mp] + res

    scores(0, sa_scr, pma_scr)

    def pair(kb):
        scores(kb + 1, sb_scr, pmb_scr)
        consume(kb, sa_scr, pma_scr, False)
        scores(kb + 2, sa_scr, pma_scr)
        consume(kb + 1, sb_scr, pmb_scr, False)

    def body(j, c):
        pair(4 * j)
        pair(4 * j + 2)
        return c
    lax.fori_loop(0, qi // 2, body, 0)

    @pl.when(qi % 2 == 1)
    def _():
        pair(2 * qi - 2)

    consume(2 * qi, sa_scr, pma_scr, True)

    kb = 2 * qi + 1
    rows_k = pl.ds(pl.multiple_of(kb * tk, tk), tk)
    q_late = q_ref[tk:, :]
    vt = vt_scr[kb]
    r = lax.broadcasted_iota(jnp.int32, (tk, tk), 0)
    c = lax.broadcasted_iota(jnp.int32, (tk, tk), 1)
    for mp in range(2):
        s = lax.dot_general(k_refs[mp][rows_k, :], q_late, nt, preferred_element_type=F32)
        s = jnp.where(r <= c, s, -jnp.inf)
        sb_scr[mp, :, 0:tk] = s
        m_old = m_scr[mp, :, tk:]
        col_max = jnp.max(jnp.max(s.reshape(tk // 8, 8, tk), axis=0), axis=0, keepdims=True)
        m_new = jnp.maximum(m_old, jnp.broadcast_to(col_max, (8, tk)))
        m_scr[mp, :, tk:] = m_new
        corr = jnp.exp2(m_old - m_new)
        m_rows = jnp.concatenate([m_new] * (ATTN_ROW_STRIP // 8), axis=0)
        for r0 in range(0, tk, ATTN_ROW_STRIP):
            rs = pl.ds(r0, ATTN_ROW_STRIP)
            p_scr[mp, rs, 0:tk] = jnp.exp2(sb_scr[mp, rs, 0:tk] - m_rows).astype(BF16)
        res = jnp.dot(vt, p_scr[mp, :, 0:tk], preferred_element_type=F32)
        acc_scr[mp, :, tk:] = (jnp.concatenate([corr] * (v_rows // 8), axis=0) * acc_scr[mp, :, tk:] + res)

    lam = (jnp.exp(jnp.sum(lq1_ref[...] * lk1_ref[...], axis=-1, keepdims=True))
           - jnp.exp(jnp.sum(lq2_ref[...] * lk2_ref[...], axis=-1, keepdims=True)) + lambda_init)

    def normalised(mp):
        denom = jnp.concatenate([acc_scr[mp, LANES:LANES + 8, :]] * (LANES // 8), axis=0)
        return acc_scr[mp, 0:LANES, :] / denom

    o = jnp.transpose(normalised(0) - lam * normalised(1))
    o = o * lax.rsqrt(jnp.mean(o * o, axis=-1, keepdims=True) + RMS_EPS)
    o_ref[...] = (o * g_ref[...] * (1.0 - lambda_init)).astype(BF16)


def _attention(q, k1, k2, v, lq1, lk1, lq2, lk2, subln_g, later_weights, *, batch, seq, lambda_init):
    n, width = q.shape
    heads = width // LANES
    tk = min(ATTN_KV_TILE, seq // 2)
    tq = 2 * tk
    nq = seq // tq
    v_rows = LANES + 16
    kv_spec = pl.BlockSpec((seq, LANES), lambda b, h, i: (b, h))
    lam_spec = pl.BlockSpec((1, DIFF_HEAD_DIM), lambda b, h, i: (0, 0))
    q_spec = pl.BlockSpec((tq, LANES), lambda b, h, i: (b * nq + i, h))
    kern = functools.partial(_attn_kernel, tq=tq, tk=tk, lambda_init=lambda_init)
    row = lambda a: a.reshape(1, -1).astype(F32)
    score_buf = pltpu.VMEM((2, tk, tq), F32)
    stat_buf = pltpu.VMEM((2, 8, tq), F32)
    cast_specs, cast_shapes = _cast_specs(later_weights, batch * heads * nq,
                                          lambda b, h, i: (b * heads + h) * nq + i)
    out, *casts = pl.pallas_call(
        kern,
        grid=(batch, heads, nq),
        in_specs=[q_spec, kv_spec, kv_spec, kv_spec, lam_spec, lam_spec, lam_spec, lam_spec,
                  pl.BlockSpec((1, LANES), lambda b, h, i: (0, 0))] + cast_specs,
        out_specs=[q_spec] + cast_specs,
        out_shape=[jax.ShapeDtypeStruct((n, width), BF16)] + cast_shapes,
        scratch_shapes=[pltpu.VMEM((seq // tk, v_rows, tk), BF16), score_buf, score_buf, stat_buf, stat_buf,
                        pltpu.VMEM((2, tk, tq), BF16), stat_buf, pltpu.VMEM((2, v_rows, tq), F32)],
        compiler_params=_params("parallel", "parallel", "arbitrary"),
        name="attn",
    )(q, k1, k2, v, row(lq1), row(lk1), row(lq2), row(lk2), row(subln_g), *later_weights)
    return out, casts


def _ssm_weights(a_re, a_im, log_dt, b_re, b_im, c_re, c_im):
    g_n, p_n, h_n = b_re.shape
    gpb = SSM_COL_TILE // h_n
    nb = g_n // gpb
    dt = jnp.exp(log_dt)[:, None]
    mag = jnp.exp(a_re * dt)
    lbar_re = mag * jnp.cos(a_im * dt)
    lbar_im = mag * jnp.sin(a_im * dt)
    nr, ni = lbar_re - 1.0, lbar_im
    den = jnp.square(a_re) + jnp.square(a_im)
    coef_re = ((nr * a_re + ni * a_im) / den)[..., None]
    coef_im = ((ni * a_re - nr * a_im) / den)[..., None]
    bb_re = coef_re * b_re - coef_im * b_im
    bb_im = coef_re * b_im + coef_im * b_re
    def block_diag(compact, blk_rows, blk_cols):
        tiled = jnp.tile(compact, (1, 1, gpb))
        rows = lax.broadcasted_iota(jnp.int32, tiled.shape, 1) // blk_rows
        cols = lax.broadcasted_iota(jnp.int32, tiled.shape, 2) // blk_cols
        return jnp.where(rows == cols, tiled, 0.0).astype(BF16)

    def in_diag(bb):
        m = bb.reshape(nb, gpb, p_n, h_n).transpose(0, 1, 3, 2).reshape(nb, gpb * h_n, p_n)
        return block_diag(m, h_n, p_n)

    def out_diag(c):
        m = c.reshape(nb, gpb, h_n, p_n).transpose(0, 1, 3, 2).reshape(nb, gpb * p_n, h_n)
        return block_diag(m, p_n, h_n)

    lam = lambda v: v.reshape(nb, 1, gpb * p_n)
    return (in_diag(bb_re), in_diag(bb_im), out_diag(c_re), out_diag(-c_im), lam(lbar_re), lam(lbar_im))


def _ssm_kernel(s_ref, sn_ref, bre_ref, bim_ref, cre_ref, cim_ref, lre_ref, lim_ref, w_ref, y_ref, w_out,
                slab_in, slab_out, sp_scr, xa_re, xa_im, xb_re, xb_im, loc_re, loc_im, ent_re, ent_im,
                hre, him, st_re, st_im, *, tt, t_len):
    i = pl.program_id(2)
    nchunk = tt // t_len
    n_slab = s_ref.shape[1] // LANES
    width = xa_re.shape[1]
    col_tile = 2 * LANES
    _cast_blocks((w_ref,), (w_out,))

    def project(src_ref, xre, xim):
        for k in range(n_slab):
            slab_in[k] = src_ref[:, k * LANES:(k + 1) * LANES]
        for k in range(n_slab):
            for r in range(t_len):
                sp_scr[r * nchunk:(r + 1) * nchunk, k * LANES:(k + 1) * LANES] = (
                    slab_in[k, pl.ds(r, nchunk, stride=t_len), :].astype(BF16))
        sp = sp_scr[...]
        xre[...] = jnp.dot(sp, bre_ref[0], preferred_element_type=F32)
        xim[...] = jnp.dot(sp, bim_ref[0], preferred_element_type=F32)

    def recur(xre, xim):
        def advance(lr, li, hr, hi, rows, lanes):
            return (lr * hr - li * hi + xre[rows, lanes], lr * hi + li * hr + xim[rows, lanes])

        for lb in range(width // LANES):
            lanes = slice(lb * LANES, (lb + 1) * LANES)
            lr, li = lre_ref[0, :, lanes], lim_ref[0, :, lanes]
            hr, hi = xre[0:nchunk, lanes], xim[0:nchunk, lanes]
            for r in range(1, t_len):
                hr, hi = advance(lr, li, hr, hi, slice(r * nchunk, (r + 1) * nchunk), lanes)
            loc_re[:, lanes] = hr
            loc_im[:, lanes] = hi

        dr, di = lre_ref[0], lim_ref[0]
        for _ in range(t_len.bit_length() - 1):
            dr, di = dr * dr - di * di, 2.0 * dr * di

        def chunk_step(c, carry):
            sr, si = carry
            row = pl.ds(c, 1)
            ent_re[row, :] = sr
            ent_im[row, :] = si
            return (dr * sr - di * si + loc_re[row, :], dr * si + di * sr + loc_im[row, :])

        sr, si = lax.fori_loop(0, nchunk, chunk_step, (st_re[...], st_im[...]), unroll=True)
        st_re[...] = sr
        st_im[...] = si

        yp = None
        for n in range(width // col_tile):
            for lb in range(n * col_tile // LANES, (n + 1) * col_tile // LANES):
                lanes = slice(lb * LANES, (lb + 1) * LANES)
                lr, li = lre_ref[0, :, lanes], lim_ref[0, :, lanes]
                hr, hi = ent_re[:, lanes], ent_im[:, lanes]
                for r in range(t_len):
                    rows = slice(r * nchunk, (r + 1) * nchunk)
                    hr, hi = advance(lr, li, hr, hi, rows, lanes)
                    hre[rows, lanes] = hr.astype(BF16)
                    him[rows, lanes] = hi.astype(BF16)
            cols = slice(n * col_tile, (n + 1) * col_tile)
            part = (jnp.dot(hre[:, cols], cre_ref[0, cols, :], preferred_element_type=F32)
                    + jnp.dot(him[:, cols], cim_ref[0, cols, :], preferred_element_type=F32))
            yp = part if yp is None else yp + part
        for k in range(n_slab):
            for r in range(t_len):
                slab_out[k, pl.ds(r, nchunk, stride=t_len), :] = yp[r * nchunk:(r + 1) * nchunk,
                                                                  k * LANES:(k + 1) * LANES]
            y_ref[:, k * LANES:(k + 1) * LANES] = slab_out[k]

    @pl.when(i == 0)
    def _():
        st_re[...] = jnp.zeros(st_re.shape, F32)
        st_im[...] = jnp.zeros(st_im.shape, F32)
        project(s_ref, xa_re, xa_im)

    @pl.when(i % 2 == 0)
    def _():
        project(sn_ref, xb_re, xb_im)
        recur(xa_re, xa_im)

    @pl.when(i % 2 == 1)
    def _():
        project(sn_ref, xa_re, xa_im)
        recur(xb_re, xb_im)


def _ssm(s2d, wts, later_weight, *, batch, seq):
    bre, bim, cre, cim, lre, lim = wts
    n, width = s2d.shape
    tt = min(SSM_TOKEN_TILE, seq)
    t_len = SSM_CHUNK
    assert t_len & (t_len - 1) == 0 and tt % t_len == 0 and seq % tt == 0
    tiles = seq // tt
    nchunk = tt // t_len
    col = SSM_COL_TILE
    state_w = bre.shape[2]
    tok_spec = pl.BlockSpec((tt, col), lambda b, j, i: (b * tiles + i, j))
    next_spec = pl.BlockSpec((tt, col), lambda b, j, i: (b * tiles + jnp.minimum(i + 1, tiles - 1), j))
    wspec = lambda shape: pl.BlockSpec((1,) + shape, lambda b, j, i: (j, 0, 0))
    kern = functools.partial(_ssm_kernel, tt=tt, t_len=t_len)
    n_col = width // col
    cast_specs, cast_shapes = _cast_specs((later_weight,), batch * n_col * tiles,
                                          lambda b, j, i: (b * n_col + j) * tiles + i)
    return pl.pallas_call(
        kern,
        grid=(batch, n_col, tiles),
        in_specs=[tok_spec, next_spec, wspec((col, state_w)), wspec((col, state_w)),
                  wspec((state_w, col)), wspec((state_w, col)), wspec((1, state_w)), wspec((1, state_w))]
        + cast_specs,
        out_specs=[tok_spec] + cast_specs,
        out_shape=[jax.ShapeDtypeStruct((n, width), F32)] + cast_shapes,
        scratch_shapes=[pltpu.VMEM((col // LANES, tt, LANES), F32), pltpu.VMEM((col // LANES, tt, LANES), F32),
                        pltpu.VMEM((tt, col), BF16),
                        pltpu.VMEM((tt, state_w), F32), pltpu.VMEM((tt, state_w), F32),
                        pltpu.VMEM((tt, state_w), F32), pltpu.VMEM((tt, state_w), F32),
                        pltpu.VMEM((nchunk, state_w), F32), pltpu.VMEM((nchunk, state_w), F32),
                        pltpu.VMEM((nchunk, state_w), F32), pltpu.VMEM((nchunk, state_w), F32),
                        pltpu.VMEM((tt, state_w), BF16), pltpu.VMEM((tt, state_w), BF16),
                        pltpu.VMEM((1, state_w), F32), pltpu.VMEM((1, state_w), F32)],
        compiler_params=_params("parallel", "parallel", "arbitrary"),
        name="ssm",
    )(s2d, s2d, bre, bim, cre, cim, lre, lim, later_weight)


def _gelu_tanh(x):
    return 0.5 * x * (1.0 + jnp.tanh(math.sqrt(2.0 / math.pi) * (x + 0.044715 * (x * x * x))))


def _mixout_kernel(attn_ref, y_ref, s_ref, x_ref, gt_ref, dskip_ref, gluw_ref, glub_ref, wout_ref,
                   g_ref, b_ref, w_ref, o_ref, w_out, *, alpha, width):
    _cast_blocks((w_ref,), (w_out,))
    y = y_ref[...] + dskip_ref[...] * s_ref[...]
    ge = _gelu_tanh(y)
    z = jnp.dot(ge.astype(BF16), gluw_ref[...], preferred_element_type=F32) + glub_ref[...]
    ssm = (ge * jax.nn.sigmoid(z)).astype(BF16)
    h = (jnp.dot(attn_ref[...], wout_ref[0:width, :], preferred_element_type=F32)
         + jnp.dot(ssm, wout_ref[width:2 * width, :], preferred_element_type=F32))
    out = alpha * x_ref[...] + (1.0 + gt_ref[0]) * h
    o_ref[...] = _layer_norm_rows(out, g_ref[...], b_ref[...])


def _mixout(attn, y, s2d, x2d, gate, d_skip, glu_w, glu_b, w_out, ln_g, ln_b, later_weight, *, seq, alpha):
    n, d = x2d.shape
    width = attn.shape[1]
    tm = min(PROJ_TOKEN_TILE, seq)
    tiles_per_seq = seq // tm
    half = pl.BlockSpec((tm, width), lambda i: (i, 0))
    full = pl.BlockSpec((tm, d), lambda i: (i, 0))
    cast_specs, cast_shapes = _cast_specs((later_weight,), n // tm, lambda i: i)
    return pl.pallas_call(
        functools.partial(_mixout_kernel, alpha=alpha, width=width),
        grid=(n // tm,),
        in_specs=[half, half, half, full,
                  pl.BlockSpec((1, 1, d), lambda i: (i // tiles_per_seq, 0, 0)),
                  _const_spec((1, width)), _const_spec(glu_w.shape), _const_spec((1, width)),
                  _const_spec(w_out.shape), _const_spec((1, d)), _const_spec((1, d))] + cast_specs,
        out_specs=[full] + cast_specs,
        out_shape=[jax.ShapeDtypeStruct((n, d), F32)] + cast_shapes,
        compiler_params=_params("parallel"),
        name="mixout",
    )(attn, y, s2d, x2d, gate, d_skip.reshape(1, width), glu_w, glu_b.reshape(1, width), w_out,
      ln_g.reshape(1, d), ln_b.reshape(1, d), later_weight)


def kernel(x, c, w_cond, b_cond, ffn1_w1, ffn1_w3, ffn1_w2, w_in, lambda_q1, lambda_k1, lambda_q2, lambda_k2, subln_g, ssm_a_re, ssm_a_im, ssm_log_dt, ssm_b_re, ssm_b_im, ssm_c_re, ssm_c_im, ssm_d, glu_w, glu_b, w_out, ffn2_w1, ffn2_w3, ffn2_w2, ln_g, ln_b):
    batch, seq, d = x.shape
    depth = w_cond.shape[0]
    alpha = (2 * depth) ** 0.25
    h = x.reshape(batch * seq, d)
    for i in range(depth):
        lambda_init = 0.8 - 0.6 * math.exp(-0.3 * i)
        mod, (f1w1, f1w3, f1w2, w_inb) = _cond(c, w_cond[i], b_cond[i],
                                               (ffn1_w1[i], ffn1_w3[i], ffn1_w2[i], w_in[i]))
        mod = mod.reshape(batch, N_SUBLAYERS, 3, 1, d)
        shift, scale, gate = (lambda k: mod[:, k, 0]), (lambda k: mod[:, k, 1]), (lambda k: mod[:, k, 2])

        h = _ffn(h, shift(0), scale(0), gate(0), f1w1, f1w3, f1w2, ln_g[i, 0], ln_b[i, 0], seq=seq, alpha=alpha)

        q, k1, k2, v, s = _inproj(h, shift(1), scale(1), w_inb, seq=seq)
        attn, (w1b, w3b, woutb) = _attention(
            q, k1, k2, v, lambda_q1[i], lambda_k1[i], lambda_q2[i], lambda_k2[i], subln_g[i],
            (ffn2_w1[i], ffn2_w3[i], w_out[i]), batch=batch, seq=seq, lambda_init=lambda_init)
        wts = _ssm_weights(ssm_a_re[i], ssm_a_im[i], ssm_log_dt[i], ssm_b_re[i], ssm_b_im[i],
                           ssm_c_re[i], ssm_c_im[i])
        y, gluwb = _ssm(s, wts, glu_w[i], batch=batch, seq=seq)
        h, w2b = _mixout(attn, y, s, h, gate(1), ssm_d[i], gluwb, glu_b[i], woutb,
                         ln_g[i, 1], ln_b[i, 1], ffn2_w2[i], seq=seq, alpha=alpha)

        h = _ffn(h, shift(2), scale(2), gate(2), w1b, w3b, w2b, ln_g[i, 2], ln_b[i, 2], seq=seq, alpha=alpha)
    return h.reshape(batch, seq, d)
```
